```python
import jax, jax.numpy as jnp
from jax import lax
import numpy as np

D_MODEL = 1024
BATCH = 32
SEQ = 2048
DEPTH = 1
DEC_BATCH = 16
DEC_SEQ = 64
PAST_LEN = 1024

CHUNK = 64
N_META = 16
GLA_HEADS = 4
GLA_DK = D_MODEL // 2
GLA_DV = D_MODEL
GLA_DK_HEAD = GLA_DK // GLA_HEADS
GLA_DV_HEAD = GLA_DV // GLA_HEADS
GATE_RANK = 16
GATE_TAU = 16.0
POOL_WIDTH = D_MODEL // 2
POOL_WINDOWS = (2, 4, 8, 16)
POOL_GROUPS = len(POOL_WINDOWS)
POOL_GROUP = POOL_WIDTH // POOL_GROUPS
POOL_OUT_GROUP = D_MODEL // POOL_GROUPS
POOL_BUF = max(POOL_WINDOWS) - 1
D_FF = -(-8 * D_MODEL // (3 * 256)) * 256
IN_SPLITS = (GLA_DK, GLA_DK, GLA_DV, GLA_DV, GATE_RANK, POOL_WIDTH, D_MODEL, D_MODEL)
D_IN = sum(IN_SPLITS)
EPS = 1e-6

kernel_name = 'gla_pool_gated_hybrid_stream_step'


def _rmsnorm(x, g):
    xf = x.astype(jnp.float32)
    y = xf * lax.rsqrt(jnp.mean(xf * xf, axis=-1, keepdims=True) + EPS)
    return (y * g.astype(jnp.float32)).astype(x.dtype)


def _gla_scan(q, k, v, g, S0):
    C = q.shape[2]
    causal = jnp.tril(jnp.ones((C, C), dtype=bool))

    def step(S, inp):
        qc, kc, vc, gc = inp
        b = jnp.cumsum(gc, axis=1)
        o_inter = jnp.einsum('bthk,bhkv->bthv', qc * jnp.exp(b), S)
        diff = b[:, :, None] - b[:, None]
        decay = jnp.exp(jnp.where(causal[None, :, :, None, None], diff, -jnp.inf))
        attn = jnp.einsum('bthk,bshk,btshk->bhts', qc, kc, decay)
        o_intra = jnp.einsum('bhts,bshv->bthv', attn, vc)
        b_last = b[:, -1]
        k_dec = kc * jnp.exp(b_last[:, None] - b)
        S_new = jnp.exp(b_last)[..., None] * S + jnp.einsum('bshk,bshv->bhkv', k_dec, vc)
        return S_new, o_inter + o_intra

    S, o = lax.scan(step, S0, (q, k, v, g))
    return o, S


def _gla(q, k, v, log_a, S0, front_pad):
    B, L = q.shape[0], q.shape[1]
    total = front_pad + L
    n_chunks = -(-total // CHUNK)
    end_pad = n_chunks * CHUNK - total

    def blocks(t):
        t = jnp.pad(t.astype(jnp.float32), ((0, 0), (front_pad, end_pad), (0, 0), (0, 0)))
        t = t.reshape(B, n_chunks, CHUNK, t.shape[2], t.shape[3])
        return jnp.moveaxis(t, 1, 0)

    o, S = _gla_scan(blocks(q), blocks(k), blocks(v), blocks(log_a), S0.astype(jnp.float32))
    o = jnp.moveaxis(o, 0, 1).reshape(B, n_chunks * CHUNK, GLA_HEADS, GLA_DV_HEAD)
    return o[:, front_pad:front_pad + L], S


def _pool_mix(p, buf, pos0):
    B, L, W = p.shape
    z = jnp.concatenate([buf, p], axis=1)
    cs = jnp.concatenate([jnp.zeros((B, 1, W), jnp.float32), jnp.cumsum(z, axis=1)], axis=1)
    pos = pos0 + jnp.arange(L)
    outs = []
    for gi, w in enumerate(POOL_WINDOWS):
        lo, hi = gi * POOL_GROUP, (gi + 1) * POOL_GROUP
        s = cs[:, POOL_BUF + 1:POOL_BUF + 1 + L, lo:hi] - cs[:, POOL_BUF + 1 - w:POOL_BUF + 1 - w + L, lo:hi]
        cnt = jnp.minimum(w, pos + 1).astype(jnp.float32)[None, :, None]
        outs.append(s / cnt - p[..., lo:hi])
    mixed = jnp.stack(outs, axis=2)
    return mixed, z[:, -POOL_BUF:]


def _mixer(xn, S0, pool_buf, pool_pos0, front_pad, w_in, w_alpha2, b_alpha, gla_norm_g, w_pool, pool_scale, w_out):
    B, L, _ = xn.shape
    proj = xn @ w_in
    cuts = [int(c) for c in np.cumsum(IN_SPLITS)[:-1]]
    q, k, v, r, a_lr, p, g_a, g_b = jnp.split(proj, cuts, axis=-1)
    q = q.reshape(B, L, GLA_HEADS, GLA_DK_HEAD) * (GLA_DK_HEAD ** -0.5)
    k = k.reshape(B, L, GLA_HEADS, GLA_DK_HEAD)
    v = v.reshape(B, L, GLA_HEADS, GLA_DV_HEAD)
    log_a = jax.nn.log_sigmoid((a_lr @ w_alpha2 + b_alpha).astype(jnp.float32)) / GATE_TAU
    log_a = log_a.reshape(B, L, GLA_HEADS, GLA_DK_HEAD)
    o, S = _gla(q, k, v, log_a, S0, front_pad)
    o = o * lax.rsqrt(jnp.mean(o * o, axis=-1, keepdims=True) + EPS)
    o = o.reshape(B, L, GLA_DV) * gla_norm_g.astype(jnp.float32)
    y_a = jax.nn.silu(r.astype(jnp.float32)) * o
    mixed, new_buf = _pool_mix(p.astype(jnp.float32), pool_buf.astype(jnp.float32), pool_pos0)
    y_b = jnp.einsum('blgc,gcd->blgd', mixed, w_pool.astype(jnp.float32)).reshape(B, L, D_MODEL)
    y_b = y_b * pool_scale.astype(jnp.float32)
    merged = jax.nn.sigmoid(g_a.astype(jnp.float32)) * y_a + jax.nn.sigmoid(g_b.astype(jnp.float32)) * y_b
    out = (merged @ w_out.astype(jnp.float32)).astype(xn.dtype)
    return out, S, new_buf


def _layer(x, S0, pool_buf, pool_pos0, front_pad, n_drop, norm1_g, w_in, w_alpha2, b_alpha, gla_norm_g,
           w_pool, pool_scale, w_out, norm2_g, w_ffn_gate, w_ffn_up, w_ffn_down, norm_f_g):
    mix, S, buf = _mixer(_rmsnorm(x, norm1_g), S0, pool_buf, pool_pos0, front_pad,
                         w_in, w_alpha2, b_alpha, gla_norm_g, w_pool, pool_scale, w_out)
    h = (x + mix)[:, n_drop:]
    hn = _rmsnorm(h, norm2_g)
    h = h + (jax.nn.silu(hn @ w_ffn_gate) * (hn @ w_ffn_up)) @ w_ffn_down
    return _rmsnorm(h, norm_f_g), S, buf


def setup_inputs(seed: int = 0) -> dict:
    key = jax.random.key(seed)
    ks = jax.random.split(key, 18)

    def n(k, shape, scale):
        return jax.random.normal(k, shape, jnp.float32) * scale

    return {
        'x_prompt': n(ks[0], (BATCH, SEQ, D_MODEL), 1.0),
        'x_sample': n(ks[1], (DEC_BATCH, DEC_SEQ, D_MODEL), 1.0),
        'state_gla': n(ks[2], (DEC_BATCH, GLA_HEADS, GLA_DK_HEAD, GLA_DV_HEAD), 1.0),
        'state_pool': n(ks[3], (DEC_BATCH, POOL_BUF, POOL_WIDTH), 1.0),
        'meta_tokens': n(ks[4], (N_META, D_MODEL), 1.0),
        'norm1_g': 1.0 + n(ks[5], (D_MODEL,), 0.02),
        'w_in': n(ks[6], (D_MODEL, D_IN), D_MODEL ** -0.5),
        'w_alpha2': n(ks[7], (GATE_RANK, GLA_DK), GATE_RANK ** -0.5),
        'b_alpha': n(ks[8], (GLA_DK,), 0.1),
        'gla_norm_g': 1.0 + n(ks[9], (GLA_DV,), 0.02),
        'w_pool': n(ks[10], (POOL_GROUPS, POOL_GROUP, POOL_OUT_GROUP), POOL_GROUP ** -0.5),
        'pool_scale': 1.0 + n(ks[11], (D_MODEL,), 0.1),
        'w_out': n(ks[12], (D_MODEL, D_MODEL), D_MODEL ** -0.5),
        'norm2_g': 1.0 + n(ks[13], (D_MODEL,), 0.02),
        'w_ffn_gate': n(ks[14], (D_MODEL, D_FF), D_MODEL ** -0.5),
        'w_ffn_up': n(ks[15], (D_MODEL, D_FF), D_MODEL ** -0.5),
        'w_ffn_down': n(ks[16], (D_FF, D_MODEL), D_FF ** -0.5),
        'norm_f_g': 1.0 + n(ks[17], (D_MODEL,), 0.02),
    }


def reference(x_prompt, x_sample, state_gla, state_pool, meta_tokens, norm1_g, w_in, w_alpha2, b_alpha,
              gla_norm_g, w_pool, pool_scale, w_out, norm2_g, w_ffn_gate, w_ffn_up, w_ffn_down, norm_f_g):
    B = x_prompt.shape[0]
    meta = jnp.broadcast_to(meta_tokens[None].astype(x_prompt.dtype), (B, N_META, D_MODEL))
    xp = jnp.concatenate([meta, x_prompt], axis=1)
    S0 = jnp.zeros((B, GLA_HEADS, GLA_DK_HEAD, GLA_DV_HEAD), jnp.float32)
    buf0 = jnp.zeros((B, POOL_BUF, POOL_WIDTH), jnp.float32)
    y_prompt, S_p, buf_p = _layer(xp, S0, buf0, 0, (-N_META) % CHUNK, N_META, norm1_g, w_in, w_alpha2, b_alpha,
                                  gla_norm_g, w_pool, pool_scale, w_out, norm2_g, w_ffn_gate, w_ffn_up,
                                  w_ffn_down, norm_f_g)
    y_sample, S_s, buf_s = _layer(x_sample, state_gla, state_pool, PAST_LEN, 0, 0, norm1_g, w_in, w_alpha2, b_alpha,
                                  gla_norm_g, w_pool, pool_scale, w_out, norm2_g, w_ffn_gate, w_ffn_up,
                                  w_ffn_down, norm_f_g)
    return (y_prompt, y_sample,
            S_p.astype(state_gla.dtype), buf_p.astype(state_pool.dtype),
            S_s.astype(state_gla.dtype), buf_s.astype(state_pool.dtype))
```

```python
import functools
import math

import numpy as np
import jax
import jax.numpy as jnp
from jax.experimental import pallas as pl
from jax.experimental.pallas import tpu as pltpu

D_MODEL = 1024
GLA_HEADS = 4
GLA_DK = D_MODEL // 2
GLA_DV = D_MODEL
DK_HEAD = GLA_DK // GLA_HEADS
DV_HEAD = GLA_DV // GLA_HEADS
GATE_RANK = 16
GATE_TAU = 16.0
POOL_WIDTH = D_MODEL // 2
POOL_WINDOWS = (2, 4, 8, 16)
POOL_GROUPS = len(POOL_WINDOWS)
POOL_GROUP = POOL_WIDTH // POOL_GROUPS
POOL_OUT_GROUP = D_MODEL // POOL_GROUPS
POOL_BUF = max(POOL_WINDOWS) - 1
POOL_CARRY = POOL_BUF + 1
PAST_LEN = 1024
EPS = 1e-6

LANES = 128
GATE_PAD = LANES

C_Q = 0
C_K = C_Q + GLA_DK
C_V = C_K + GLA_DK
C_R = C_V + GLA_DV
C_P = C_R + GLA_DV
C_GA = C_P + POOL_WIDTH
C_GB = C_GA + D_MODEL
C_A = C_GB + D_MODEL
D_IN_PACKED = C_A + GATE_PAD

VMEM_LIMIT_BYTES = 56 * 1024 * 1024

PROMPT_TILE = 256
META_TILE = 128
FFN_TILE = 512

F32 = jnp.float32
BF16 = jnp.bfloat16


def _decay_tables(T):
    J = int(math.log2(T))
    assert 1 << J == T
    t = np.arange(T)[:, None]
    u = np.arange(T)[None, :]
    mats = [(u <= t), (u > t)]
    for j in range(J):
        m = 1 << j
        lo = (t // (2 * m)) * (2 * m)
        ref = lo + m - 1
        upper = t >= lo + m
        mats.append(np.where(upper, (u > ref) & (u <= t), (u > t) & (u <= ref)))
    l_all = np.stack(mats).astype(np.float32)
    x = t ^ u
    hb = np.floor(np.log2(np.maximum(x, 1))).astype(np.int32)
    lvl = np.where(u > t, -1, np.where(u == t, J, hb)).astype(np.int32)
    return J, l_all, lvl


def _pool_tables(T):
    t = np.arange(T)[:, None]
    u = np.arange(T)[None, :]
    cur, car = [], []
    tc = np.arange(POOL_CARRY)[:, None]
    c = np.arange(POOL_CARRY)[None, :]
    for w in POOL_WINDOWS:
        d = t - u
        cur.append(np.where((d >= 0) & (d < w), 1.0 / w, 0.0) - (d == 0))
        dc = tc + POOL_CARRY - c
        car.append(np.where((dc < w) & (c >= 1), 1.0 / w, 0.0))
    return np.stack(cur).astype(np.float32), np.stack(car).astype(np.float32)


def _rms(x, g):
    return x * jax.lax.rsqrt(jnp.mean(x * x, axis=-1, keepdims=True) + EPS) * g


def _dot(a, b):
    return jnp.dot(a, b, preferred_element_type=F32)


def _dot_nt(a, b):
    return jax.lax.dot_general(a, b, (((1,), (1,)), ((), ())), preferred_element_type=F32)


def _dot_tn(a, b):
    return jax.lax.dot_general(a, b, (((0,), (0,)), ((), ())), preferred_element_type=F32)


def _log_sigmoid(z):
    return -(jnp.maximum(-z, 0.0) + jnp.log1p(jnp.exp(-jnp.abs(z))))


def _mixer_kernel(x_ref, s0_ref, buf0_ref, n1_ref, win_ref, wa2_ref, ba_ref, gng_ref,
                  wpool_ref, pscale_ref, wout_ref, lall_ref, lvl_ref, mcur_ref, mcar_ref,
                  h_ref, s_ref, buf_ref,
                  e_scr, q_scr, k_scr, mixed_scr, merged_scr, *, T, J):
    @pl.when(pl.program_id(1) == 0)
    def _():
        s_ref[0] = s0_ref[0]
        buf_ref[0] = buf0_ref[0]

    x = x_ref[0]
    xn = _rms(x, n1_ref[...]).astype(BF16)

    def proj(lo, width):
        return _dot(xn, win_ref[:, lo:lo + width])

    a_lr = proj(C_A, GATE_PAD).astype(BF16)
    g = _log_sigmoid(_dot(a_lr, wa2_ref[...]) + ba_ref[...]) * (1.0 / GATE_TAU)
    g_hi = g.astype(BF16)
    g_lo = (g - g_hi.astype(F32)).astype(BF16)
    g2 = jnp.concatenate([g_hi, g_lo], axis=1)

    for i in range(J + 2):
        s = _dot(lall_ref[i], g2)
        e_scr[i] = jnp.exp(s[:, :GLA_DK] + s[:, GLA_DK:])

    q_scr[...] = proj(C_Q, GLA_DK) * (DK_HEAD ** -0.5)
    k_scr[...] = proj(C_K, GLA_DK)

    p = proj(C_P, POOL_WIDTH)
    p_bf = p.astype(BF16)
    carry_bf = buf_ref[0].astype(BF16)
    for gi in range(POOL_GROUPS):
        cs = slice(gi * POOL_GROUP, (gi + 1) * POOL_GROUP)
        mixed_scr[:, cs] = _dot(mcur_ref[gi], p_bf[:, cs])
        mixed_scr[0:POOL_CARRY, cs] += _dot(mcar_ref[gi], carry_bf[:, cs])
    buf_ref[0] = p[T - POOL_CARRY:, :]

    lvl = lvl_ref[...]
    ones_tv = jnp.ones((T, DV_HEAD), BF16)
    for h in range(GLA_HEADS):
        ks = slice(h * DK_HEAD, (h + 1) * DK_HEAD)
        vs = slice(h * DV_HEAD, (h + 1) * DV_HEAD)
        qh = q_scr[:, ks]
        kh = k_scr[:, ks]
        vh = proj(C_V + h * DV_HEAD, DV_HEAD).astype(BF16)

        att = jnp.where(lvl == J, _dot_nt(qh.astype(BF16), kh.astype(BF16)), 0.0)
        for j in range(J):
            e = e_scr[2 + j, :, ks]
            pj = _dot_nt((qh * e).astype(BF16), (kh * e).astype(BF16))
            att = jnp.where(lvl == j, pj, att)

        s_h = s_ref[0, h]
        o = _dot((qh * e_scr[0, :, ks]).astype(BF16), s_h.astype(BF16))
        o = o + _dot(att.astype(BF16), vh)

        k_dec = (kh * e_scr[1, :, ks]).astype(BF16)
        tot = _dot_tn(g_hi[:, ks], ones_tv) + _dot_tn(g_lo[:, ks], ones_tv)
        s_ref[0, h] = jnp.exp(tot) * s_h + _dot_tn(k_dec, vh)

        o = o * jax.lax.rsqrt(jnp.mean(o * o, axis=-1, keepdims=True) + EPS) * gng_ref[:, vs]
        r = proj(C_R + h * DV_HEAD, DV_HEAD)
        y_a = r * jax.nn.sigmoid(r) * o

        y_b = _dot(mixed_scr[:, ks].astype(BF16), wpool_ref[h]) * pscale_ref[:, vs]

        g_a = proj(C_GA + h * DV_HEAD, DV_HEAD)
        g_b = proj(C_GB + h * DV_HEAD, DV_HEAD)
        merged = jax.nn.sigmoid(g_a) * y_a + jax.nn.sigmoid(g_b) * y_b
        merged_scr[:, vs] = merged.astype(BF16)

    h_ref[0] = x + _dot(merged_scr[...], wout_ref[...])


def _const_spec(shape):
    zeros = (0,) * len(shape)
    return pl.BlockSpec(shape, lambda b, t: zeros, pipeline_mode=pl.Buffered(1))


def _mixer(x, s0, buf0, w, T, shared_state):
    B, L, D = x.shape
    assert L % T == 0 and T >= POOL_CARRY
    J, l_all, lvl = _decay_tables(T)
    m_cur, m_car = _pool_tables(T)
    consts = (jnp.asarray(l_all, BF16), jnp.asarray(lvl), jnp.asarray(m_cur, BF16), jnp.asarray(m_car, BF16))

    state_idx = (lambda b, t: (0, 0, 0, 0)) if shared_state else (lambda b, t: (b, 0, 0, 0))
    buf_idx = (lambda b, t: (0, 0, 0)) if shared_state else (lambda b, t: (b, 0, 0))
    weights = (w["norm1_g"], w["w_in"], w["w_alpha2"], w["b_alpha"], w["gla_norm_g"],
               w["w_pool"], w["pool_scale"], w["w_out"]) + consts
    in_specs = [
        pl.BlockSpec((1, T, D), lambda b, t: (b, t, 0)),
        pl.BlockSpec((1, GLA_HEADS, DK_HEAD, DV_HEAD), state_idx),
        pl.BlockSpec((1, POOL_CARRY, POOL_WIDTH), buf_idx),
    ] + [_const_spec(a.shape) for a in weights]
    out_shape = (
        jax.ShapeDtypeStruct((B, L, D), F32),
        jax.ShapeDtypeStruct((B, GLA_HEADS, DK_HEAD, DV_HEAD), F32),
        jax.ShapeDtypeStruct((B, POOL_CARRY, POOL_WIDTH), F32),
    )
    out_specs = (
        pl.BlockSpec((1, T, D), lambda b, t: (b, t, 0)),
        pl.BlockSpec((1, GLA_HEADS, DK_HEAD, DV_HEAD), lambda b, t: (b, 0, 0, 0)),
        pl.BlockSpec((1, POOL_CARRY, POOL_WIDTH), lambda b, t: (b, 0, 0)),
    )
    scratch = [
        pltpu.VMEM((J + 2, T, GLA_DK), F32),
        pltpu.VMEM((T, GLA_DK), F32),
        pltpu.VMEM((T, GLA_DK), F32),
        pltpu.VMEM((T, POOL_WIDTH), F32),
        pltpu.VMEM((T, D_MODEL), BF16),
    ]
    return pl.pallas_call(
        functools.partial(_mixer_kernel, T=T, J=J),
        out_shape=out_shape,
        grid=(B, L // T),
        in_specs=in_specs,
        out_specs=out_specs,
        scratch_shapes=scratch,
        compiler_params=pltpu.CompilerParams(
            dimension_semantics=("arbitrary", "arbitrary"),
            vmem_limit_bytes=VMEM_LIMIT_BYTES),
        name=f"mixer_t{T}",
    )(x, s0, buf0, *weights)


def _ffn_kernel(h_ref, n2_ref, wg_ref, wu_ref, wd_ref, nf_ref, y_ref):
    h = h_ref[...]
    hn = _rms(h, n2_ref[...]).astype(BF16)
    gate = _dot(hn, wg_ref[...])
    up = _dot(hn, wu_ref[...])
    act = (gate * jax.nn.sigmoid(gate) * up).astype(BF16)
    h2 = h + _dot(act, wd_ref[...])
    y_ref[...] = _rms(h2, nf_ref[...])


def _ffn(h, w, R):
    N, D = h.shape
    assert N % R == 0
    weights = (w["norm2_g"], w["w_ffn_gate"], w["w_ffn_up"], w["w_ffn_down"], w["norm_f_g"])

    def const(a):
        zeros = (0,) * a.ndim
        return pl.BlockSpec(a.shape, lambda i: zeros, pipeline_mode=pl.Buffered(1))

    return pl.pallas_call(
        _ffn_kernel,
        out_shape=jax.ShapeDtypeStruct((N, D), F32),
        grid=(N // R,),
        in_specs=[pl.BlockSpec((R, D), lambda i: (i, 0))] + [const(a) for a in weights],
        out_specs=pl.BlockSpec((R, D), lambda i: (i, 0)),
        compiler_params=pltpu.CompilerParams(
            dimension_semantics=("arbitrary",),
            vmem_limit_bytes=VMEM_LIMIT_BYTES),
        name=f"ffn_r{R}",
    )(h, *weights)


def _pack_weights(norm1_g, w_in, w_alpha2, b_alpha, gla_norm_g, w_pool, pool_scale, w_out,
                  norm2_g, w_ffn_gate, w_ffn_up, w_ffn_down, norm_f_g):
    o_a = 2 * GLA_DK + 2 * GLA_DV
    o_p = o_a + GATE_RANK
    w_in_packed = jnp.concatenate(
        [w_in[:, :o_a], w_in[:, o_p:], w_in[:, o_a:o_p],
         jnp.zeros((D_MODEL, GATE_PAD - GATE_RANK), w_in.dtype)], axis=1).astype(BF16)
    w_alpha2_packed = jnp.concatenate(
        [w_alpha2, jnp.zeros((GATE_PAD - GATE_RANK, GLA_DK), w_alpha2.dtype)], axis=0).astype(BF16)
    row = lambda v: v.reshape(1, -1).astype(F32)
    return dict(
        norm1_g=row(norm1_g), w_in=w_in_packed, w_alpha2=w_alpha2_packed, b_alpha=row(b_alpha),
        gla_norm_g=row(gla_norm_g), w_pool=w_pool.astype(BF16), pool_scale=row(pool_scale),
        w_out=w_out.astype(BF16), norm2_g=row(norm2_g), w_ffn_gate=w_ffn_gate.astype(BF16),
        w_ffn_up=w_ffn_up.astype(BF16), w_ffn_down=w_ffn_down.astype(BF16), norm_f_g=row(norm_f_g))


def _largest_tile(n, cap):
    t = cap
    while n % t:
        t //= 2
    return t


def kernel(x_prompt, x_sample, state_gla, state_pool, meta_tokens, norm1_g, w_in, w_alpha2, b_alpha,
           gla_norm_g, w_pool, pool_scale, w_out, norm2_g, w_ffn_gate, w_ffn_up, w_ffn_down, norm_f_g):
    w = _pack_weights(norm1_g, w_in, w_alpha2, b_alpha, gla_norm_g, w_pool, pool_scale, w_out,
                      norm2_g, w_ffn_gate, w_ffn_up, w_ffn_down, norm_f_g)
    B, L, D = x_prompt.shape
    Bs, Ls, _ = x_sample.shape
    n_meta = meta_tokens.shape[0]
    assert POOL_BUF <= n_meta <= META_TILE and PAST_LEN >= POOL_BUF

    s_zero = jnp.zeros((1, GLA_HEADS, DK_HEAD, DV_HEAD), F32)
    buf_zero = jnp.zeros((1, POOL_CARRY, POOL_WIDTH), F32)
    x_meta = jnp.pad(meta_tokens.astype(F32), ((META_TILE - n_meta, 0), (0, 0)))[None]
    _, s_meta, buf_meta = _mixer(x_meta, s_zero, buf_zero, w, META_TILE, True)

    h_p, s_p, buf_p = _mixer(x_prompt, s_meta, buf_meta, w, _largest_tile(L, PROMPT_TILE), True)

    buf_s0 = jnp.pad(state_pool.astype(F32), ((0, 0), (POOL_CARRY - POOL_BUF, 0), (0, 0)))
    h_s, s_s, buf_s = _mixer(x_sample, state_gla.astype(F32), buf_s0, w, _largest_tile(Ls, PROMPT_TILE), False)

    y_p = _ffn(h_p.reshape(B * L, D), w, _largest_tile(B * L, FFN_TILE)).reshape(B, L, D)
    y_s = _ffn(h_s.reshape(Bs * Ls, D), w, _largest_tile(Bs * Ls, FFN_TILE)).reshape(Bs, Ls, D)
    return (y_p, y_s,
            s_p.astype(state_gla.dtype), buf_p[:, POOL_CARRY - POOL_BUF:].astype(state_pool.dtype),
            s_s.astype(state_gla.dtype), buf_s[:, POOL_CARRY - POOL_BUF:].astype(state_pool.dtype))
```

```python
import functools
import math

import numpy as np
import jax
import jax.numpy as jnp
from jax.experimental import pallas as pl
from jax.experimental.pallas import tpu as pltpu

D_MODEL = 1024
GLA_HEADS = 4
GLA_DK = D_MODEL // 2
GLA_DV = D_MODEL
DK_HEAD = GLA_DK // GLA_HEADS
DV_HEAD = GLA_DV // GLA_HEADS
GATE_RANK = 16
GATE_TAU = 16.0
POOL_WIDTH = D_MODEL // 2
POOL_WINDOWS = (2, 4, 8, 16)
POOL_GROUPS = len(POOL_WINDOWS)
POOL_GROUP = POOL_WIDTH // POOL_GROUPS
POOL_OUT_GROUP = D_MODEL // POOL_GROUPS
POOL_BUF = max(POOL_WINDOWS) - 1
POOL_CARRY = POOL_BUF + 1
PAST_LEN = 1024
EPS = 1e-6

LANES = 128
SUBLANES = 8
GATE_PAD = LANES

C_Q = 0
C_K = C_Q + GLA_DK
C_V = C_K + GLA_DK
C_R = C_V + GLA_DV
C_P = C_R + GLA_DV
C_GA = C_P + POOL_WIDTH
C_GB = C_GA + D_MODEL
C_A = C_GB + D_MODEL
D_IN_PACKED = C_A + GATE_PAD

VMEM_LIMIT_BYTES = 56 * 1024 * 1024

PROMPT_TILE = 256
META_TILE = 128
FFN_TILE = 512

F32 = jnp.float32
BF16 = jnp.bfloat16


def _level_table(T):
    J = int(math.log2(T))
    assert 1 << J == T and T >= 2 * SUBLANES
    t = np.arange(T)[:, None]
    u = np.arange(T)[None, :]
    hb = np.floor(np.log2(np.maximum(t ^ u, 1))).astype(np.int32)
    lvl = np.where(u > t, -1, np.where(u == t, J, hb)).astype(np.int32)
    return J, lvl


def _pool_tables(T):
    t = np.arange(T)[:, None]
    u = np.arange(T)[None, :]
    cur, car = [], []
    tc = np.arange(POOL_CARRY)[:, None]
    c = np.arange(POOL_CARRY)[None, :]
    for w in POOL_WINDOWS:
        d = t - u
        cur.append(np.where((d >= 0) & (d < w), 1.0 / w, 0.0) - (d == 0))
        dc = tc + POOL_CARRY - c
        car.append(np.where((dc < w) & (c >= 1), 1.0 / w, 0.0))
    return np.stack(cur).astype(np.float32), np.stack(car).astype(np.float32)


def _rms(x, g):
    return x * jax.lax.rsqrt(jnp.mean(x * x, axis=-1, keepdims=True) + EPS) * g


def _dot(a, b):
    return jnp.dot(a, b, preferred_element_type=F32)


def _dot_nt(a, b):
    return jax.lax.dot_general(a, b, (((1,), (1,)), ((), ())), preferred_element_type=F32)


def _dot_tn(a, b):
    return jax.lax.dot_general(a, b, (((0,), (0,)), ((), ())), preferred_element_type=F32)


def _log_sigmoid(z):
    return -(jnp.maximum(-z, 0.0) + jnp.log1p(jnp.exp(-jnp.abs(z))))


def _decay_factors(g, e_scr, ep_scr, es_scr, T, J):
    a = jnp.exp(g).reshape(T // SUBLANES, SUBLANES, GLA_DK)
    r = jax.lax.broadcasted_iota(jnp.int32, a.shape, 1)
    one = jnp.ones_like(a)

    def prev(x, s):
        return pltpu.roll(x, s, axis=1)

    def nxt(x, s):
        return pltpu.roll(x, SUBLANES - s, axis=1)

    s1 = a * jnp.where(r >= 1, prev(a, 1), one)
    s2 = s1 * jnp.where(r >= 2, prev(s1, 2), one)
    ep = s2 * jnp.where(r >= 4, prev(s2, 4), one)
    c = jnp.where(r <= 6, nxt(a, 1), one)
    d1 = c * jnp.where(r <= 6, nxt(c, 1), one)
    d2 = d1 * jnp.where(r <= 5, nxt(d1, 2), one)
    es = d2 * jnp.where(r <= 3, nxt(d2, 4), one)

    r4 = r & 3
    e0 = jnp.where((r & 1) == 1, a, one)
    e1 = jnp.where(r4 == 0, c, jnp.where(r4 == 1, one, jnp.where(r4 == 2, a, s1)))
    lower2 = jnp.where(r == 0, d1 * nxt(c, 2), jnp.where(r == 1, d1, jnp.where(r == 2, c, one)))
    upper2 = jnp.where(r == 4, a, jnp.where(r == 5, s1, jnp.where(r == 6, s1 * prev(a, 2), s2)))
    e2 = jnp.where(r <= 3, lower2, upper2)
    for j, e in enumerate((e0, e1, e2)):
        e_scr[j] = e.reshape(T, GLA_DK)
    ep_scr[...] = ep.reshape(T, GLA_DK)
    es_scr[...] = es.reshape(T, GLA_DK)

    for j in range(3, J):
        m = 1 << j
        for lo in range(0, T, 2 * m):
            mid, hi = lo + m, lo + 2 * m
            e_scr[j, lo:mid] = es_scr[lo:mid]
            e_scr[j, mid:hi] = ep_scr[mid:hi]
            lower_total = ep_scr[mid - 1:mid, :]
            upper_total = ep_scr[hi - 1:hi, :]
            ep_scr[mid:hi] = ep_scr[mid:hi] * lower_total
            es_scr[lo:mid] = es_scr[lo:mid] * upper_total


def _mixer_kernel(x_ref, s0_ref, buf0_ref, n1_ref, win_ref, wa2_ref, ba_ref, gng_ref,
                  wpool_ref, pscale_ref, wout_ref, lvl_ref, mcur_ref, mcar_ref,
                  h_ref, s_ref, buf_ref,
                  e_scr, ep_scr, es_scr, q_scr, k_scr, mixed_scr, merged_scr, *, T, J):
    @pl.when(pl.program_id(1) == 0)
    def _():
        s_ref[0] = s0_ref[0]
        buf_ref[0] = buf0_ref[0]

    x = x_ref[0]
    xn = _rms(x, n1_ref[...]).astype(BF16)

    def proj(lo, width):
        return _dot(xn, win_ref[:, lo:lo + width])

    a_lr = proj(C_A, GATE_PAD).astype(BF16)
    g = _log_sigmoid(_dot(a_lr, wa2_ref[...]) + ba_ref[...]) * (1.0 / GATE_TAU)
    _decay_factors(g, e_scr, ep_scr, es_scr, T, J)

    q_scr[...] = proj(C_Q, GLA_DK) * (DK_HEAD ** -0.5)
    k_scr[...] = proj(C_K, GLA_DK)

    p = proj(C_P, POOL_WIDTH)
    p_bf = p.astype(BF16)
    carry_bf = buf_ref[0].astype(BF16)
    for gi in range(POOL_GROUPS):
        cs = slice(gi * POOL_GROUP, (gi + 1) * POOL_GROUP)
        mixed_scr[:, cs] = _dot(mcur_ref[gi], p_bf[:, cs])
        mixed_scr[0:POOL_CARRY, cs] += _dot(mcar_ref[gi], carry_bf[:, cs])
    buf_ref[0] = p[T - POOL_CARRY:, :]

    lvl = lvl_ref[...]
    for h in range(GLA_HEADS):
        ks = slice(h * DK_HEAD, (h + 1) * DK_HEAD)
        vs = slice(h * DV_HEAD, (h + 1) * DV_HEAD)
        qh = q_scr[:, ks]
        kh = k_scr[:, ks]
        vh = proj(C_V + h * DV_HEAD, DV_HEAD).astype(BF16)

        att = jnp.where(lvl == J, _dot_nt(qh.astype(BF16), kh.astype(BF16)), 0.0)
        for j in range(J):
            e = e_scr[j, :, ks]
            pj = _dot_nt((qh * e).astype(BF16), (kh * e).astype(BF16))
            att = jnp.where(lvl == j, pj, att)

        s_h = s_ref[0, h]
        o = _dot((qh * ep_scr[:, ks]).astype(BF16), s_h.astype(BF16))
        o = o + _dot(att.astype(BF16), vh)

        k_dec = (kh * es_scr[:, ks]).astype(BF16)
        tile_decay = jnp.broadcast_to(ep_scr[T - 1:T, ks], (DK_HEAD, DK_HEAD)).T
        tile_decay = jnp.concatenate([tile_decay] * (DV_HEAD // DK_HEAD), axis=1)
        s_ref[0, h] = tile_decay * s_h + _dot_tn(k_dec, vh)

        o = o * jax.lax.rsqrt(jnp.mean(o * o, axis=-1, keepdims=True) + EPS) * gng_ref[:, vs]
        r = proj(C_R + h * DV_HEAD, DV_HEAD)
        y_a = r * jax.nn.sigmoid(r) * o

        y_b = _dot(mixed_scr[:, ks].astype(BF16), wpool_ref[h]) * pscale_ref[:, vs]

        g_a = proj(C_GA + h * DV_HEAD, DV_HEAD)
        g_b = proj(C_GB + h * DV_HEAD, DV_HEAD)
        merged = jax.nn.sigmoid(g_a) * y_a + jax.nn.sigmoid(g_b) * y_b
        merged_scr[:, vs] = merged.astype(BF16)

    h_ref[0] = x + _dot(merged_scr[...], wout_ref[...])


def _const_spec(shape):
    zeros = (0,) * len(shape)
    return pl.BlockSpec(shape, lambda b, t: zeros, pipeline_mode=pl.Buffered(1))


def _mixer(x, s0, buf0, w, T, shared_state):
    B, L, D = x.shape
    assert L % T == 0 and T >= POOL_CARRY
    J, lvl = _level_table(T)
    m_cur, m_car = _pool_tables(T)
    consts = (jnp.asarray(lvl), jnp.asarray(m_cur, BF16), jnp.asarray(m_car, BF16))

    state_idx = (lambda b, t: (0, 0, 0, 0)) if shared_state else (lambda b, t: (b, 0, 0, 0))
    buf_idx = (lambda b, t: (0, 0, 0)) if shared_state else (lambda b, t: (b, 0, 0))
    weights = (w["norm1_g"], w["w_in"], w["w_alpha2"], w["b_alpha"], w["gla_norm_g"],
               w["w_pool"], w["pool_scale"], w["w_out"]) + consts
    in_specs = [
        pl.BlockSpec((1, T, D), lambda b, t: (b, t, 0)),
        pl.BlockSpec((1, GLA_HEADS, DK_HEAD, DV_HEAD), state_idx),
        pl.BlockSpec((1, POOL_CARRY, POOL_WIDTH), buf_idx),
    ] + [_const_spec(a.shape) for a in weights]
    out_shape = (
        jax.ShapeDtypeStruct((B, L, D), F32),
        jax.ShapeDtypeStruct((B, GLA_HEADS, DK_HEAD, DV_HEAD), F32),
        jax.ShapeDtypeStruct((B, POOL_CARRY, POOL_WIDTH), F32),
    )
    out_specs = (
        pl.BlockSpec((1, T, D), lambda b, t: (b, t, 0)),
        pl.BlockSpec((1, GLA_HEADS, DK_HEAD, DV_HEAD), lambda b, t: (b, 0, 0, 0)),
        pl.BlockSpec((1, POOL_CARRY, POOL_WIDTH), lambda b, t: (b, 0, 0)),
    )
    scratch = [
        pltpu.VMEM((J, T, GLA_DK), F32),
        pltpu.VMEM((T, GLA_DK), F32),
        pltpu.VMEM((T, GLA_DK), F32),
        pltpu.VMEM((T, GLA_DK), F32),
        pltpu.VMEM((T, GLA_DK), F32),
        pltpu.VMEM((T, POOL_WIDTH), F32),
        pltpu.VMEM((T, D_MODEL), BF16),
    ]
    return pl.pallas_call(
        functools.partial(_mixer_kernel, T=T, J=J),
        out_shape=out_shape,
        grid=(B, L // T),
        in_specs=in_specs,
        out_specs=out_specs,
        scratch_shapes=scratch,
        compiler_params=pltpu.CompilerParams(
            dimension_semantics=("arbitrary", "arbitrary"),
            vmem_limit_bytes=VMEM_LIMIT_BYTES),
        name=f"mixer_t{T}",
    )(x, s0, buf0, *weights)


def _ffn_kernel(h_ref, n2_ref, wg_ref, wu_ref, wd_ref, nf_ref, y_ref):
    h = h_ref[...]
    hn = _rms(h, n2_ref[...]).astype(BF16)
    gate = _dot(hn, wg_ref[...])
    up = _dot(hn, wu_ref[...])
    act = (gate * jax.nn.sigmoid(gate) * up).astype(BF16)
    h2 = h + _dot(act, wd_ref[...])
    y_ref[...] = _rms(h2, nf_ref[...])


def _ffn(h, w, R):
    N, D = h.shape
    assert N % R == 0
    weights = (w["norm2_g"], w["w_ffn_gate"], w["w_ffn_up"], w["w_ffn_down"], w["norm_f_g"])

    def const(a):
        zeros = (0,) * a.ndim
        return pl.BlockSpec(a.shape, lambda i: zeros, pipeline_mode=pl.Buffered(1))

    return pl.pallas_call(
        _ffn_kernel,
        out_shape=jax.ShapeDtypeStruct((N, D), F32),
        grid=(N // R,),
        in_specs=[pl.BlockSpec((R, D), lambda i: (i, 0))] + [const(a) for a in weights],
        out_specs=pl.BlockSpec((R, D), lambda i: (i, 0)),
        compiler_params=pltpu.CompilerParams(
            dimension_semantics=("arbitrary",),
            vmem_limit_bytes=VMEM_LIMIT_BYTES),
        name=f"ffn_r{R}",
    )(h, *weights)


def _pack_weights(norm1_g, w_in, w_alpha2, b_alpha, gla_norm_g, w_pool, pool_scale, w_out,
                  norm2_g, w_ffn_gate, w_ffn_up, w_ffn_down, norm_f_g):
    o_a = 2 * GLA_DK + 2 * GLA_DV
    o_p = o_a + GATE_RANK
    w_in_packed = jnp.concatenate(
        [w_in[:, :o_a], w_in[:, o_p:], w_in[:, o_a:o_p],
         jnp.zeros((D_MODEL, GATE_PAD - GATE_RANK), w_in.dtype)], axis=1).astype(BF16)
    w_alpha2_packed = jnp.concatenate(
        [w_alpha2, jnp.zeros((GATE_PAD - GATE_RANK, GLA_DK), w_alpha2.dtype)], axis=0).astype(BF16)
    row = lambda v: v.reshape(1, -1).astype(F32)
    return dict(
        norm1_g=row(norm1_g), w_in=w_in_packed, w_alpha2=w_alpha2_packed, b_alpha=row(b_alpha),
        gla_norm_g=row(gla_norm_g), w_pool=w_pool.astype(BF16), pool_scale=row(pool_scale),
        w_out=w_out.astype(BF16), norm2_g=row(norm2_g), w_ffn_gate=w_ffn_gate.astype(BF16),
        w_ffn_up=w_ffn_up.astype(BF16), w_ffn_down=w_ffn_down.astype(BF16), norm_f_g=row(norm_f_g))


def _largest_tile(n, cap):
    t = cap
    while n % t:
        t //= 2
    return t


def kernel(x_prompt, x_sample, state_gla, state_pool, meta_tokens, norm1_g, w_in, w_alpha2, b_alpha,
           gla_norm_g, w_pool, pool_scale, w_out, norm2_g, w_ffn_gate, w_ffn_up, w_ffn_down, norm_f_g):
    w = _pack_weights(norm1_g, w_in, w_alpha2, b_alpha, gla_norm_g, w_pool, pool_scale, w_out,
                      norm2_g, w_ffn_gate, w_ffn_up, w_ffn_down, norm_f_g)
    B, L, D = x_prompt.shape
    Bs, Ls, _ = x_sample.shape
    n_meta = meta_tokens.shape[0]
    assert POOL_BUF <= n_meta <= META_TILE and PAST_LEN >= POOL_BUF

    s_zero = jnp.zeros((1, GLA_HEADS, DK_HEAD, DV_HEAD), F32)
    buf_zero = jnp.zeros((1, POOL_CARRY, POOL_WIDTH), F32)
    x_meta = jnp.pad(meta_tokens.astype(F32), ((META_TILE - n_meta, 0), (0, 0)))[None]
    _, s_meta, buf_meta = _mixer(x_meta, s_zero, buf_zero, w, META_TILE, True)

    h_p, s_p, buf_p = _mixer(x_prompt, s_meta, buf_meta, w, _largest_tile(L, PROMPT_TILE), True)

    buf_s0 = jnp.pad(state_pool.astype(F32), ((0, 0), (POOL_CARRY - POOL_BUF, 0), (0, 0)))
    h_s, s_s, buf_s = _mixer(x_sample, state_gla.astype(F32), buf_s0, w, _largest_tile(Ls, PROMPT_TILE), False)

    y_p = _ffn(h_p.reshape(B * L, D), w, _largest_tile(B * L, FFN_TILE)).reshape(B, L, D)
    y_s = _ffn(h_s.reshape(Bs * Ls, D), w, _largest_tile(Bs * Ls, FFN_TILE)).reshape(Bs, Ls, D)
    return (y_p, y_s,
            s_p.astype(state_gla.dtype), buf_p[:, POOL_CARRY - POOL_BUF:].astype(state_pool.dtype),
            s_s.astype(state_gla.dtype), buf_s[:, POOL_CARRY - POOL_BUF:].astype(state_pool.dtype))
```

```python
import functools
import math

import numpy as np
import jax
import jax.numpy as jnp
from jax.experimental import pallas as pl
from jax.experimental.pallas import tpu as pltpu

D_MODEL = 1024
GLA_HEADS = 4
GLA_DK = D_MODEL // 2
GLA_DV = D_MODEL
DK_HEAD = GLA_DK // GLA_HEADS
DV_HEAD = GLA_DV // GLA_HEADS
GATE_RANK = 16
GATE_TAU = 16.0
POOL_WIDTH = D_MODEL // 2
POOL_WINDOWS = (2, 4, 8, 16)
POOL_GROUPS = len(POOL_WINDOWS)
POOL_GROUP = POOL_WIDTH // POOL_GROUPS
POOL_OUT_GROUP = D_MODEL // POOL_GROUPS
POOL_BUF = max(POOL_WINDOWS) - 1
POOL_CARRY = POOL_BUF + 1
PAST_LEN = 1024
EPS = 1e-6

LANES = 128
SUBLANES = 8
GATE_PAD = LANES

C_Q = 0
C_K = C_Q + GLA_DK
C_V = C_K + GLA_DK
C_R = C_V + GLA_DV
W_GLA_COLS = C_R + GLA_DV
C_P = 0
C_GA = C_P + POOL_WIDTH
C_GB = C_GA + D_MODEL

VMEM_LIMIT_BYTES = 56 * 1024 * 1024

PROMPT_TILE = 256
TILES_PER_STEP = 2
META_TILE = 128
FFN_TILE = 512

F32 = jnp.float32
BF16 = jnp.bfloat16


def _level_table(T):
    J = int(math.log2(T))
    assert 1 << J == T and T >= 2 * SUBLANES
    t = np.arange(T)[:, None]
    u = np.arange(T)[None, :]
    hb = np.floor(np.log2(np.maximum(t ^ u, 1))).astype(np.int32)
    lvl = np.where(u > t, -1, np.where(u == t, J, hb)).astype(np.int32)
    return J, lvl


def _rms(x, g):
    return x * jax.lax.rsqrt(jnp.mean(x * x, axis=-1, keepdims=True) + EPS) * g


def _dot(a, b):
    return jnp.dot(a, b, preferred_element_type=F32)


def _dot_nt(a, b):
    return jax.lax.dot_general(a, b, (((1,), (1,)), ((), ())), preferred_element_type=F32)


def _dot_tn(a, b):
    return jax.lax.dot_general(a, b, (((0,), (0,)), ((), ())), preferred_element_type=F32)


def _log_sigmoid(z):
    return jnp.minimum(z, 0.0) - jnp.log(1.0 + jnp.exp(-jnp.abs(z)))


def _decay_factors(g, e_scr, ep_scr, es_scr, T, J):
    a = jnp.exp(g).reshape(T // SUBLANES, SUBLANES, GLA_DK)
    r = jax.lax.broadcasted_iota(jnp.int32, a.shape, 1)
    one = jnp.ones_like(a)

    def prev(x, s):
        return pltpu.roll(x, s, axis=1)

    def nxt(x, s):
        return pltpu.roll(x, SUBLANES - s, axis=1)

    s1 = a * jnp.where(r >= 1, prev(a, 1), one)
    s2 = s1 * jnp.where(r >= 2, prev(s1, 2), one)
    ep = s2 * jnp.where(r >= 4, prev(s2, 4), one)
    c = jnp.where(r <= 6, nxt(a, 1), one)
    d1 = c * jnp.where(r <= 6, nxt(c, 1), one)
    d2 = d1 * jnp.where(r <= 5, nxt(d1, 2), one)
    es = d2 * jnp.where(r <= 3, nxt(d2, 4), one)

    r4 = r & 3
    e0 = jnp.where((r & 1) == 1, a, one)
    e1 = jnp.where(r4 == 0, c, jnp.where(r4 == 1, one, jnp.where(r4 == 2, a, s1)))
    lower2 = jnp.where(r == 0, d1 * nxt(c, 2), jnp.where(r == 1, d1, jnp.where(r == 2, c, one)))
    upper2 = jnp.where(r == 4, a, jnp.where(r == 5, s1, jnp.where(r == 6, s1 * prev(a, 2), s2)))
    e2 = jnp.where(r <= 3, lower2, upper2)
    for j, e in enumerate((e0, e1, e2)):
        e_scr[j] = e.reshape(T, GLA_DK)
    ep_scr[...] = ep.reshape(T, GLA_DK)
    es_scr[...] = es.reshape(T, GLA_DK)

    for j in range(3, J):
        m = 1 << j
        for lo in range(0, T, 2 * m):
            mid, hi = lo + m, lo + 2 * m
            e_scr[j, lo:mid] = es_scr[lo:mid]
            e_scr[j, mid:hi] = ep_scr[mid:hi]
            lower_total = ep_scr[mid - 1:mid, :]
            upper_total = ep_scr[hi - 1:hi, :]
            ep_scr[mid:hi] = ep_scr[mid:hi] * lower_total
            es_scr[lo:mid] = es_scr[lo:mid] * upper_total


def _pool_mix(p, carry, mixed_scr, T):
    n = (T + POOL_CARRY) // SUBLANES
    s = jnp.concatenate([carry, p], axis=0).reshape(n, SUBLANES, POOL_WIDTH)
    tok = s[POOL_CARRY // SUBLANES:]
    r = jax.lax.broadcasted_iota(jnp.int32, (n, SUBLANES, LANES), 1)

    def shifted(x, k):
        down = lambda y: jnp.concatenate([y[:1], y[:-1]], axis=0)
        if k == SUBLANES:
            return down(x)
        xr = pltpu.roll(x, k, axis=1)
        rr = jnp.concatenate([r] * (x.shape[-1] // LANES), axis=-1)
        return jnp.where(rr >= k, xr, down(xr))

    width = 1
    for gi, w in enumerate(POOL_WINDOWS):
        assert w == 2 * width and w <= POOL_CARRY
        s = s + shifted(s, width)
        width = w
        mixed = s[POOL_CARRY // SUBLANES:, :, :POOL_GROUP] * (1.0 / w) - tok[:, :, gi * POOL_GROUP:(gi + 1) * POOL_GROUP]
        mixed_scr[:, gi * POOL_GROUP:(gi + 1) * POOL_GROUP] = mixed.reshape(T, POOL_GROUP)
        s = s[:, :, POOL_GROUP:]


def _mixer_tile(x, n1_ref, wgla_ref, wgate_ref, wpg_ref, wa2_ref, ba_ref, gng_ref, wpool_ref, pscale_ref,
                wout_ref, lvl_ref, s_ref, buf_ref,
                e_scr, ep_scr, es_scr, q_scr, k_scr, v_scr, r_scr, ga_scr, gb_scr, mixed_scr, merged_scr, T, J):
    xn = _rms(x, n1_ref[...]).astype(BF16)

    def proj(w_ref, lo, width):
        return _dot(xn, w_ref[:, lo:lo + width])

    a_lr = proj(wgate_ref, 0, GATE_PAD).astype(BF16)
    g = _log_sigmoid(_dot(a_lr, wa2_ref[...]) + ba_ref[...]) * (1.0 / GATE_TAU)

    q_scr[...] = proj(wgla_ref, C_Q, GLA_DK) * (DK_HEAD ** -0.5)
    k_scr[...] = proj(wgla_ref, C_K, GLA_DK)
    p = proj(wpg_ref, C_P, POOL_WIDTH)
    v_scr[...] = proj(wgla_ref, C_V, GLA_DV).astype(BF16)

    _decay_factors(g, e_scr, ep_scr, es_scr, T, J)
    _pool_mix(p, buf_ref[0], mixed_scr, T)
    buf_ref[0] = p[T - POOL_CARRY:, :]

    r = proj(wgla_ref, C_R, GLA_DV)
    r_scr[...] = r * jax.nn.sigmoid(r)
    ga_scr[...] = jax.nn.sigmoid(proj(wpg_ref, C_GA, D_MODEL))
    gb_scr[...] = jax.nn.sigmoid(proj(wpg_ref, C_GB, D_MODEL)) * pscale_ref[...]

    lvl = lvl_ref[...]
    for h in range(GLA_HEADS):
        ks = slice(h * DK_HEAD, (h + 1) * DK_HEAD)
        vs = slice(h * DV_HEAD, (h + 1) * DV_HEAD)
        qh = q_scr[:, ks]
        kh = k_scr[:, ks]
        vh = v_scr[:, vs]

        att = jnp.where(lvl == J, _dot_nt(qh.astype(BF16), kh.astype(BF16)), 0.0)
        for j in range(J):
            e = e_scr[j, :, ks]
            pj = _dot_nt((qh * e).astype(BF16), (kh * e).astype(BF16))
            att = jnp.where(lvl == j, pj, att)

        s_h = s_ref[0, h]
        o = _dot((qh * ep_scr[:, ks]).astype(BF16), s_h.astype(BF16))
        o = o + _dot(att.astype(BF16), vh)

        k_dec = (kh * es_scr[:, ks]).astype(BF16)
        tile_decay = jnp.broadcast_to(ep_scr[T - 1:T, ks], (DK_HEAD, DK_HEAD)).T
        tile_decay = jnp.concatenate([tile_decay] * (DV_HEAD // DK_HEAD), axis=1)
        s_ref[0, h] = tile_decay * s_h + _dot_tn(k_dec, vh)

        o = o * jax.lax.rsqrt(jnp.mean(o * o, axis=-1, keepdims=True) + EPS) * gng_ref[:, vs]
        y_b = _dot(mixed_scr[:, ks].astype(BF16), wpool_ref[h])
        merged = ga_scr[:, vs] * (r_scr[:, vs] * o) + gb_scr[:, vs] * y_b
        merged_scr[:, vs] = merged.astype(BF16)

    return x + _dot(merged_scr[...], wout_ref[...])


def _mixer_kernel(x_ref, s0_ref, buf0_ref, n1_ref, wgla_ref, wgate_ref, wpg_ref, wa2_ref, ba_ref, gng_ref,
                  wpool_ref, pscale_ref, wout_ref, lvl_ref,
                  h_ref, s_ref, buf_ref, *scratch, T, J, U):
    @pl.when(pl.program_id(1) == 0)
    def _():
        s_ref[0] = s0_ref[0]
        buf_ref[0] = buf0_ref[0]

    for u in range(U):
        rows = slice(u * T, (u + 1) * T)
        h_ref[0, rows, :] = _mixer_tile(
            x_ref[0, rows, :], n1_ref, wgla_ref, wgate_ref, wpg_ref, wa2_ref, ba_ref, gng_ref, wpool_ref,
            pscale_ref, wout_ref, lvl_ref, s_ref, buf_ref, *scratch, T, J)


def _const_spec(shape):
    zeros = (0,) * len(shape)
    return pl.BlockSpec(shape, lambda b, t: zeros, pipeline_mode=pl.Buffered(1))


def _mixer(x, s0, buf0, w, T, shared_state):
    B, L, D = x.shape
    assert L % T == 0 and T >= POOL_CARRY
    U = math.gcd(L // T, TILES_PER_STEP)
    J, lvl = _level_table(T)
    consts = (jnp.asarray(lvl),)

    state_idx = (lambda b, t: (0, 0, 0, 0)) if shared_state else (lambda b, t: (b, 0, 0, 0))
    buf_idx = (lambda b, t: (0, 0, 0)) if shared_state else (lambda b, t: (b, 0, 0))
    weights = (w["norm1_g"], w["w_gla"], w["w_gate"], w["w_poolgate"], w["w_alpha2"], w["b_alpha"], w["gla_norm_g"],
               w["w_pool"], w["pool_scale"], w["w_out"]) + consts
    in_specs = [
        pl.BlockSpec((1, U * T, D), lambda b, t: (b, t, 0)),
        pl.BlockSpec((1, GLA_HEADS, DK_HEAD, DV_HEAD), state_idx),
        pl.BlockSpec((1, POOL_CARRY, POOL_WIDTH), buf_idx),
    ] + [_const_spec(a.shape) for a in weights]
    out_shape = (
        jax.ShapeDtypeStruct((B, L, D), F32),
        jax.ShapeDtypeStruct((B, GLA_HEADS, DK_HEAD, DV_HEAD), F32),
        jax.ShapeDtypeStruct((B, POOL_CARRY, POOL_WIDTH), F32),
    )
    out_specs = (
        pl.BlockSpec((1, U * T, D), lambda b, t: (b, t, 0)),
        pl.BlockSpec((1, GLA_HEADS, DK_HEAD, DV_HEAD), lambda b, t: (b, 0, 0, 0)),
        pl.BlockSpec((1, POOL_CARRY, POOL_WIDTH), lambda b, t: (b, 0, 0)),
    )
    scratch = [
        pltpu.VMEM((J, T, GLA_DK), F32),
        pltpu.VMEM((T, GLA_DK), F32),
        pltpu.VMEM((T, GLA_DK), F32),
        pltpu.VMEM((T, GLA_DK), F32),
        pltpu.VMEM((T, GLA_DK), F32),
        pltpu.VMEM((T, GLA_DV), BF16),
        pltpu.VMEM((T, GLA_DV), F32),
        pltpu.VMEM((T, D_MODEL), F32),
        pltpu.VMEM((T, D_MODEL), F32),
        pltpu.VMEM((T, POOL_WIDTH), F32),
        pltpu.VMEM((T, D_MODEL), BF16),
    ]
    return pl.pallas_call(
        functools.partial(_mixer_kernel, T=T, J=J, U=U),
        out_shape=out_shape,
        grid=(B, L // (U * T)),
        in_specs=in_specs,
        out_specs=out_specs,
        scratch_shapes=scratch,
        compiler_params=pltpu.CompilerParams(
            dimension_semantics=("arbitrary", "arbitrary"),
            vmem_limit_bytes=VMEM_LIMIT_BYTES),
        name=f"mixer_t{T}x{U}",
    )(x, s0, buf0, *weights)


def _ffn_kernel(h_ref, n2_ref, wg_ref, wu_ref, wd_ref, nf_ref, y_ref):
    h = h_ref[...]
    hn = _rms(h, n2_ref[...]).astype(BF16)
    gate = _dot(hn, wg_ref[...])
    up = _dot(hn, wu_ref[...])
    act = (gate * jax.nn.sigmoid(gate) * up).astype(BF16)
    h2 = h + _dot(act, wd_ref[...])
    y_ref[...] = _rms(h2, nf_ref[...])


def _ffn(h, w, R):
    N, D = h.shape
    assert N % R == 0
    weights = (w["norm2_g"], w["w_ffn_gate"], w["w_ffn_up"], w["w_ffn_down"], w["norm_f_g"])

    def const(a):
        zeros = (0,) * a.ndim
        return pl.BlockSpec(a.shape, lambda i: zeros, pipeline_mode=pl.Buffered(1))

    return pl.pallas_call(
        _ffn_kernel,
        out_shape=jax.ShapeDtypeStruct((N, D), F32),
        grid=(N // R,),
        in_specs=[pl.BlockSpec((R, D), lambda i: (i, 0))] + [const(a) for a in weights],
        out_specs=pl.BlockSpec((R, D), lambda i: (i, 0)),
        compiler_params=pltpu.CompilerParams(
            dimension_semantics=("arbitrary",),
            vmem_limit_bytes=VMEM_LIMIT_BYTES),
        name=f"ffn_r{R}",
    )(h, *weights)


def _pack_weights(norm1_g, w_in, w_alpha2, b_alpha, gla_norm_g, w_pool, pool_scale, w_out,
                  norm2_g, w_ffn_gate, w_ffn_up, w_ffn_down, norm_f_g):
    o_p = W_GLA_COLS + GATE_RANK
    w_gla = w_in[:, :W_GLA_COLS].astype(BF16)
    w_gate = jnp.pad(w_in[:, W_GLA_COLS:o_p], ((0, 0), (0, GATE_PAD - GATE_RANK))).astype(BF16)
    w_poolgate = w_in[:, o_p:].astype(BF16)
    w_alpha2_packed = jnp.pad(w_alpha2, ((0, GATE_PAD - GATE_RANK), (0, 0))).astype(BF16)
    row = lambda v: v.reshape(1, -1).astype(F32)
    return dict(
        norm1_g=row(norm1_g), w_gla=w_gla, w_gate=w_gate, w_poolgate=w_poolgate,
        w_alpha2=w_alpha2_packed, b_alpha=row(b_alpha),
        gla_norm_g=row(gla_norm_g), w_pool=w_pool.astype(BF16), pool_scale=row(pool_scale),
        w_out=w_out.astype(BF16), norm2_g=row(norm2_g), w_ffn_gate=w_ffn_gate.astype(BF16),
        w_ffn_up=w_ffn_up.astype(BF16), w_ffn_down=w_ffn_down.astype(BF16), norm_f_g=row(norm_f_g))


def _largest_tile(n, cap):
    t = cap
    while n % t:
        t //= 2
    return t


def kernel(x_prompt, x_sample, state_gla, state_pool, meta_tokens, norm1_g, w_in, w_alpha2, b_alpha,
           gla_norm_g, w_pool, pool_scale, w_out, norm2_g, w_ffn_gate, w_ffn_up, w_ffn_down, norm_f_g):
    w = _pack_weights(norm1_g, w_in, w_alpha2, b_alpha, gla_norm_g, w_pool, pool_scale, w_out,
                      norm2_g, w_ffn_gate, w_ffn_up, w_ffn_down, norm_f_g)
    B, L, D = x_prompt.shape
    Bs, Ls, _ = x_sample.shape
    n_meta = meta_tokens.shape[0]
    assert POOL_BUF <= n_meta <= META_TILE and PAST_LEN >= POOL_BUF

    s_zero = jnp.zeros((1, GLA_HEADS, DK_HEAD, DV_HEAD), F32)
    buf_zero = jnp.zeros((1, POOL_CARRY, POOL_WIDTH), F32)
    x_meta = jnp.pad(meta_tokens.astype(F32), ((META_TILE - n_meta, 0), (0, 0)))[None]
    _, s_meta, buf_meta = _mixer(x_meta, s_zero, buf_zero, w, META_TILE, True)

    h_p, s_p, buf_p = _mixer(x_prompt, s_meta, buf_meta, w, _largest_tile(L, PROMPT_TILE), True)

    buf_s0 = jnp.pad(state_pool.astype(F32), ((0, 0), (POOL_CARRY - POOL_BUF, 0), (0, 0)))
    h_s, s_s, buf_s = _mixer(x_sample, state_gla.astype(F32), buf_s0, w, _largest_tile(Ls, PROMPT_TILE), False)

    y_p = _ffn(h_p.reshape(B * L, D), w, _largest_tile(B * L, FFN_TILE)).reshape(B, L, D)
    y_s = _ffn(h_s.reshape(Bs * Ls, D), w, _largest_tile(Bs * Ls, FFN_TILE)).reshape(Bs, Ls, D)
    return (y_p, y_s,
            s_p.astype(state_gla.dtype), buf_p[:, POOL_CARRY - POOL_BUF:].astype(state_pool.dtype),
            s_s.astype(state_gla.dtype), buf_s[:, POOL_CARRY - POOL_BUF:].astype(state_pool.dtype))
```

```python
import functools
import math

import numpy as np
import jax
import jax.numpy as jnp
from jax.experimental import pallas as pl
from jax.experimental.pallas import tpu as pltpu

D_MODEL = 1024
GLA_HEADS = 4
GLA_DK = D_MODEL // 2
GLA_DV = D_MODEL
DK_HEAD = GLA_DK // GLA_HEADS
DV_HEAD = GLA_DV // GLA_HEADS
GATE_RANK = 16
GATE_TAU = 16.0
POOL_WIDTH = D_MODEL // 2
POOL_WINDOWS = (2, 4, 8, 16)
POOL_GROUPS = len(POOL_WINDOWS)
POOL_GROUP = POOL_WIDTH // POOL_GROUPS
POOL_OUT_GROUP = D_MODEL // POOL_GROUPS
POOL_BUF = max(POOL_WINDOWS) - 1
POOL_CARRY = POOL_BUF + 1
PAST_LEN = 1024
EPS = 1e-6

LANES = 128
SUBLANES = 8
GATE_PAD = LANES

C_Q = 0
C_K = C_Q + GLA_DK
C_V = C_K + GLA_DK
C_R = C_V + GLA_DV
W_GLA_COLS = C_R + GLA_DV
C_P = 0
C_GA = C_P + POOL_WIDTH
C_GB = C_GA + D_MODEL

VMEM_LIMIT_BYTES = 56 * 1024 * 1024

PROMPT_TILE = 256
TILES_PER_STEP = 2
META_TILE = 128
FFN_TILE = 512
FFN_PIECES = 2

F32 = jnp.float32
BF16 = jnp.bfloat16


def _level_table(T):
    J = int(math.log2(T))
    assert 1 << J == T and T >= 2 * SUBLANES
    t = np.arange(T)[:, None]
    u = np.arange(T)[None, :]
    hb = np.floor(np.log2(np.maximum(t ^ u, 1))).astype(np.int32)
    lvl = np.where(u > t, -1, np.where(u == t, J, hb)).astype(np.int32)
    return J, lvl


def _rms(x, g):
    return x * jax.lax.rsqrt(jnp.mean(x * x, axis=-1, keepdims=True) + EPS) * g


def _dot(a, b):
    return jnp.dot(a, b, preferred_element_type=F32)


def _dot_nt(a, b):
    return jax.lax.dot_general(a, b, (((1,), (1,)), ((), ())), preferred_element_type=F32)


def _dot_tn(a, b):
    return jax.lax.dot_general(a, b, (((0,), (0,)), ((), ())), preferred_element_type=F32)


def _log_sigmoid(z):
    return jnp.minimum(z, 0.0) - jnp.log(1.0 + jnp.exp(-jnp.abs(z)))


def _decay_factors(g, e_scr, ep_scr, es_scr, T, J):
    a = jnp.exp(g).reshape(T // SUBLANES, SUBLANES, GLA_DK)
    r = jax.lax.broadcasted_iota(jnp.int32, a.shape, 1)
    one = jnp.ones_like(a)

    def prev(x, s):
        return pltpu.roll(x, s, axis=1)

    def nxt(x, s):
        return pltpu.roll(x, SUBLANES - s, axis=1)

    s1 = a * jnp.where(r >= 1, prev(a, 1), one)
    s2 = s1 * jnp.where(r >= 2, prev(s1, 2), one)
    ep = s2 * jnp.where(r >= 4, prev(s2, 4), one)
    c = jnp.where(r <= 6, nxt(a, 1), one)
    d1 = c * jnp.where(r <= 6, nxt(c, 1), one)
    d2 = d1 * jnp.where(r <= 5, nxt(d1, 2), one)
    es = d2 * jnp.where(r <= 3, nxt(d2, 4), one)

    r4 = r & 3
    e0 = jnp.where((r & 1) == 1, a, one)
    e1 = jnp.where(r4 == 0, c, jnp.where(r4 == 1, one, jnp.where(r4 == 2, a, s1)))
    lower2 = jnp.where(r == 0, d1 * nxt(c, 2), jnp.where(r == 1, d1, jnp.where(r == 2, c, one)))
    upper2 = jnp.where(r == 4, a, jnp.where(r == 5, s1, jnp.where(r == 6, s1 * prev(a, 2), s2)))
    e2 = jnp.where(r <= 3, lower2, upper2)
    for j, e in enumerate((e0, e1, e2)):
        e_scr[j] = e.reshape(T, GLA_DK)
    ep_scr[...] = ep.reshape(T, GLA_DK)
    es_scr[...] = es.reshape(T, GLA_DK)

    for j in range(3, J):
        m = 1 << j
        for lo in range(0, T, 2 * m):
            mid, hi = lo + m, lo + 2 * m
            e_scr[j, lo:mid] = es_scr[lo:mid]
            e_scr[j, mid:hi] = ep_scr[mid:hi]
            lower_total = ep_scr[mid - 1:mid, :]
            upper_total = ep_scr[hi - 1:hi, :]
            ep_scr[mid:hi] = ep_scr[mid:hi] * lower_total
            es_scr[lo:mid] = es_scr[lo:mid] * upper_total


def _pool_mix(p, carry, mixed_scr, T):
    n = (T + POOL_CARRY) // SUBLANES
    s = jnp.concatenate([carry, p], axis=0).reshape(n, SUBLANES, POOL_WIDTH)
    tok = s[POOL_CARRY // SUBLANES:]
    r = jax.lax.broadcasted_iota(jnp.int32, (n, SUBLANES, LANES), 1)

    def shifted(x, k):
        down = lambda y: jnp.concatenate([y[:1], y[:-1]], axis=0)
        if k == SUBLANES:
            return down(x)
        xr = pltpu.roll(x, k, axis=1)
        rr = jnp.concatenate([r] * (x.shape[-1] // LANES), axis=-1)
        return jnp.where(rr >= k, xr, down(xr))

    width = 1
    for gi, w in enumerate(POOL_WINDOWS):
        assert w == 2 * width and w <= POOL_CARRY
        s = s + shifted(s, width)
        width = w
        mixed = s[POOL_CARRY // SUBLANES:, :, :POOL_GROUP] * (1.0 / w) - tok[:, :, gi * POOL_GROUP:(gi + 1) * POOL_GROUP]
        mixed_scr[:, gi * POOL_GROUP:(gi + 1) * POOL_GROUP] = mixed.reshape(T, POOL_GROUP)
        s = s[:, :, POOL_GROUP:]


def _tile_front(x, n1_ref, wgla_ref, wgate_ref, wpg_ref, wa2_ref, ba_ref, buf_ref,
                e_scr, ep_scr, es_scr, q_scr, k_scr, v_scr, r_scr, ga_scr, gb_scr, mixed_scr, merged_scr, T, J):
    xn = _rms(x, n1_ref[...]).astype(BF16)

    def proj(w_ref, lo, width):
        return _dot(xn, w_ref[:, lo:lo + width])

    a_lr = proj(wgate_ref, 0, GATE_PAD).astype(BF16)
    g = _log_sigmoid(_dot(a_lr, wa2_ref[...]) + ba_ref[...]) * (1.0 / GATE_TAU)

    q_scr[...] = proj(wgla_ref, C_Q, GLA_DK)
    k_scr[...] = proj(wgla_ref, C_K, GLA_DK)
    p = proj(wpg_ref, C_P, POOL_WIDTH)
    v_scr[...] = proj(wgla_ref, C_V, GLA_DV)
    r_scr[...] = proj(wgla_ref, C_R, GLA_DV)
    ga_scr[...] = proj(wpg_ref, C_GA, D_MODEL)
    gb_scr[...] = proj(wpg_ref, C_GB, D_MODEL)

    _decay_factors(g, e_scr, ep_scr, es_scr, T, J)
    _pool_mix(p, buf_ref[0], mixed_scr, T)
    buf_ref[0] = p[T - POOL_CARRY:, :]


def _tile_back(x, gng_ref, wpool_ref, pscale_ref, wout_ref, lvl_ref, s_ref,
               e_scr, ep_scr, es_scr, q_scr, k_scr, v_scr, r_scr, ga_scr, gb_scr, mixed_scr, merged_scr, T, J):
    lvl = lvl_ref[...]
    for h in range(GLA_HEADS):
        ks = slice(h * DK_HEAD, (h + 1) * DK_HEAD)
        vs = slice(h * DV_HEAD, (h + 1) * DV_HEAD)
        qh = q_scr[:, ks] * (DK_HEAD ** -0.5)
        kh = k_scr[:, ks]
        vh = v_scr[:, vs].astype(BF16)

        att = jnp.where(lvl == J, _dot_nt(qh.astype(BF16), kh.astype(BF16)), 0.0)
        for j in range(J):
            e = e_scr[j, :, ks]
            pj = _dot_nt((qh * e).astype(BF16), (kh * e).astype(BF16))
            att = jnp.where(lvl == j, pj, att)

        s_h = s_ref[0, h]
        o = _dot((qh * ep_scr[:, ks]).astype(BF16), s_h.astype(BF16))
        o = o + _dot(att.astype(BF16), vh)

        k_dec = (kh * es_scr[:, ks]).astype(BF16)
        tile_decay = jnp.broadcast_to(ep_scr[T - 1:T, ks], (DK_HEAD, DK_HEAD)).T
        tile_decay = jnp.concatenate([tile_decay] * (DV_HEAD // DK_HEAD), axis=1)
        s_ref[0, h] = tile_decay * s_h + _dot_tn(k_dec, vh)

        o = o * jax.lax.rsqrt(jnp.mean(o * o, axis=-1, keepdims=True) + EPS) * gng_ref[:, vs]
        y_b = _dot(mixed_scr[:, ks].astype(BF16), wpool_ref[h])
        r = r_scr[:, vs]
        y_a = r * jax.nn.sigmoid(r) * o
        merged = jax.nn.sigmoid(ga_scr[:, vs]) * y_a + jax.nn.sigmoid(gb_scr[:, vs]) * (y_b * pscale_ref[:, vs])
        merged_scr[:, vs] = merged.astype(BF16)

    return x + _dot(merged_scr[...], wout_ref[...])


def _mixer_kernel(x_ref, s0_ref, buf0_ref, n1_ref, wgla_ref, wgate_ref, wpg_ref, wa2_ref, ba_ref, gng_ref,
                  wpool_ref, pscale_ref, wout_ref, lvl_ref,
                  h_ref, s_ref, buf_ref, *scratch, T, J, U):
    @pl.when(pl.program_id(1) == 0)
    def _():
        s_ref[0] = s0_ref[0]
        buf_ref[0] = buf0_ref[0]

    tiles = [[scr.at[u] for scr in scratch] for u in range(U)]
    rows = [slice(u * T, (u + 1) * T) for u in range(U)]
    for u in range(U):
        _tile_front(x_ref[0, rows[u], :], n1_ref, wgla_ref, wgate_ref, wpg_ref, wa2_ref, ba_ref, buf_ref,
                    *tiles[u], T, J)
    for u in range(U):
        h_ref[0, rows[u], :] = _tile_back(
            x_ref[0, rows[u], :], gng_ref, wpool_ref, pscale_ref, wout_ref, lvl_ref, s_ref,
            *tiles[u], T, J)


def _const_spec(shape):
    zeros = (0,) * len(shape)
    return pl.BlockSpec(shape, lambda b, t: zeros, pipeline_mode=pl.Buffered(1))


def _mixer(x, s0, buf0, w, T, shared_state):
    B, L, D = x.shape
    assert L % T == 0 and T >= POOL_CARRY
    U = math.gcd(L // T, TILES_PER_STEP)
    J, lvl = _level_table(T)
    consts = (jnp.asarray(lvl),)

    state_idx = (lambda b, t: (0, 0, 0, 0)) if shared_state else (lambda b, t: (b, 0, 0, 0))
    buf_idx = (lambda b, t: (0, 0, 0)) if shared_state else (lambda b, t: (b, 0, 0))
    weights = (w["norm1_g"], w["w_gla"], w["w_gate"], w["w_poolgate"], w["w_alpha2"], w["b_alpha"], w["gla_norm_g"],
               w["w_pool"], w["pool_scale"], w["w_out"]) + consts
    in_specs = [
        pl.BlockSpec((1, U * T, D), lambda b, t: (b, t, 0)),
        pl.BlockSpec((1, GLA_HEADS, DK_HEAD, DV_HEAD), state_idx),
        pl.BlockSpec((1, POOL_CARRY, POOL_WIDTH), buf_idx),
    ] + [_const_spec(a.shape) for a in weights]
    out_shape = (
        jax.ShapeDtypeStruct((B, L, D), F32),
        jax.ShapeDtypeStruct((B, GLA_HEADS, DK_HEAD, DV_HEAD), F32),
        jax.ShapeDtypeStruct((B, POOL_CARRY, POOL_WIDTH), F32),
    )
    out_specs = (
        pl.BlockSpec((1, U * T, D), lambda b, t: (b, t, 0)),
        pl.BlockSpec((1, GLA_HEADS, DK_HEAD, DV_HEAD), lambda b, t: (b, 0, 0, 0)),
        pl.BlockSpec((1, POOL_CARRY, POOL_WIDTH), lambda b, t: (b, 0, 0)),
    )
    scratch = [
        pltpu.VMEM((U, J, T, GLA_DK), F32),
        pltpu.VMEM((U, T, GLA_DK), F32),
        pltpu.VMEM((U, T, GLA_DK), F32),
        pltpu.VMEM((U, T, GLA_DK), F32),
        pltpu.VMEM((U, T, GLA_DK), F32),
        pltpu.VMEM((U, T, GLA_DV), F32),
        pltpu.VMEM((U, T, GLA_DV), F32),
        pltpu.VMEM((U, T, D_MODEL), F32),
        pltpu.VMEM((U, T, D_MODEL), F32),
        pltpu.VMEM((U, T, POOL_WIDTH), F32),
        pltpu.VMEM((U, T, D_MODEL), BF16),
    ]
    return pl.pallas_call(
        functools.partial(_mixer_kernel, T=T, J=J, U=U),
        out_shape=out_shape,
        grid=(B, L // (U * T)),
        in_specs=in_specs,
        out_specs=out_specs,
        scratch_shapes=scratch,
        compiler_params=pltpu.CompilerParams(
            dimension_semantics=("arbitrary", "arbitrary"),
            vmem_limit_bytes=VMEM_LIMIT_BYTES),
        name=f"mixer_t{T}x{U}",
    )(x, s0, buf0, *weights)


def _ffn_kernel(h_ref, n2_ref, wg_ref, wu_ref, wd_ref, nf_ref, y_ref):
    R = h_ref.shape[0] // FFN_PIECES
    rows = [slice(i * R, (i + 1) * R) for i in range(FFN_PIECES)]
    hs = [h_ref[r, :] for r in rows]
    hns = [_rms(h, n2_ref[...]).astype(BF16) for h in hs]
    for i in range(FFN_PIECES):
        gate = _dot(hns[i], wg_ref[...])
        up = _dot(hns[i], wu_ref[...])
        act = (gate * jax.nn.sigmoid(gate) * up).astype(BF16)
        h2 = hs[i] + _dot(act, wd_ref[...])
        y_ref[rows[i], :] = _rms(h2, nf_ref[...])


def _ffn(h, w, R):
    N, D = h.shape
    assert N % R == 0
    weights = (w["norm2_g"], w["w_ffn_gate"], w["w_ffn_up"], w["w_ffn_down"], w["norm_f_g"])

    def const(a):
        zeros = (0,) * a.ndim
        return pl.BlockSpec(a.shape, lambda i: zeros, pipeline_mode=pl.Buffered(1))

    return pl.pallas_call(
        _ffn_kernel,
        out_shape=jax.ShapeDtypeStruct((N, D), F32),
        grid=(N // R,),
        in_specs=[pl.BlockSpec((R, D), lambda i: (i, 0))] + [const(a) for a in weights],
        out_specs=pl.BlockSpec((R, D), lambda i: (i, 0)),
        compiler_params=pltpu.CompilerParams(
            dimension_semantics=("arbitrary",),
            vmem_limit_bytes=VMEM_LIMIT_BYTES),
        name=f"ffn_r{R}",
    )(h, *weights)


def _pack_weights(norm1_g, w_in, w_alpha2, b_alpha, gla_norm_g, w_pool, pool_scale, w_out,
                  norm2_g, w_ffn_gate, w_ffn_up, w_ffn_down, norm_f_g):
    o_p = W_GLA_COLS + GATE_RANK
    w_gla = w_in[:, :W_GLA_COLS].astype(BF16)
    w_gate = jnp.pad(w_in[:, W_GLA_COLS:o_p], ((0, 0), (0, GATE_PAD - GATE_RANK))).astype(BF16)
    w_poolgate = w_in[:, o_p:].astype(BF16)
    w_alpha2_packed = jnp.pad(w_alpha2, ((0, GATE_PAD - GATE_RANK), (0, 0))).astype(BF16)
    row = lambda v: v.reshape(1, -1).astype(F32)
    return dict(
        norm1_g=row(norm1_g), w_gla=w_gla, w_gate=w_gate, w_poolgate=w_poolgate,
        w_alpha2=w_alpha2_packed, b_alpha=row(b_alpha),
        gla_norm_g=row(gla_norm_g), w_pool=w_pool.astype(BF16), pool_scale=row(pool_scale),
        w_out=w_out.astype(BF16), norm2_g=row(norm2_g), w_ffn_gate=w_ffn_gate.astype(BF16),
        w_ffn_up=w_ffn_up.astype(BF16), w_ffn_down=w_ffn_down.astype(BF16), norm_f_g=row(norm_f_g))


def _largest_tile(n, cap):
    t = cap
    while n % t:
        t //= 2
    return t


def kernel(x_prompt, x_sample, state_gla, state_pool, meta_tokens, norm1_g, w_in, w_alpha2, b_alpha,
           gla_norm_g, w_pool, pool_scale, w_out, norm2_g, w_ffn_gate, w_ffn_up, w_ffn_down, norm_f_g):
    w = _pack_weights(norm1_g, w_in, w_alpha2, b_alpha, gla_norm_g, w_pool, pool_scale, w_out,
                      norm2_g, w_ffn_gate, w_ffn_up, w_ffn_down, norm_f_g)
    B, L, D = x_prompt.shape
    Bs, Ls, _ = x_sample.shape
    n_meta = meta_tokens.shape[0]
    assert POOL_BUF <= n_meta <= META_TILE and PAST_LEN >= POOL_BUF

    s_zero = jnp.zeros((1, GLA_HEADS, DK_HEAD, DV_HEAD), F32)
    buf_zero = jnp.zeros((1, POOL_CARRY, POOL_WIDTH), F32)
    x_meta = jnp.pad(meta_tokens.astype(F32), ((META_TILE - n_meta, 0), (0, 0)))[None]
    _, s_meta, buf_meta = _mixer(x_meta, s_zero, buf_zero, w, META_TILE, True)

    h_p, s_p, buf_p = _mixer(x_prompt, s_meta, buf_meta, w, _largest_tile(L, PROMPT_TILE), True)

    buf_s0 = jnp.pad(state_pool.astype(F32), ((0, 0), (POOL_CARRY - POOL_BUF, 0), (0, 0)))
    h_s, s_s, buf_s = _mixer(x_sample, state_gla.astype(F32), buf_s0, w, _largest_tile(Ls, PROMPT_TILE), False)

    y_p = _ffn(h_p.reshape(B * L, D), w, _largest_tile(B * L, FFN_TILE)).reshape(B, L, D)
    y_s = _ffn(h_s.reshape(Bs * Ls, D), w, _largest_tile(Bs * Ls, FFN_TILE)).reshape(Bs, Ls, D)
    return (y_p, y_s,
            s_p.astype(state_gla.dtype), buf_p[:, POOL_CARRY - POOL_BUF:].astype(state_pool.dtype),
            s_s.astype(state_gla.dtype), buf_s[:, POOL_CARRY - POOL_BUF:].astype(state_pool.dtype))
```

```python
import functools
import math

import numpy as np
import jax
import jax.numpy as jnp
from jax.experimental import pallas as pl
from jax.experimental.pallas import tpu as pltpu

D_MODEL = 1024
GLA_HEADS = 4
GLA_DK = D_MODEL // 2
GLA_DV = D_MODEL
DK_HEAD = GLA_DK // GLA_HEADS
DV_HEAD = GLA_DV // GLA_HEADS
GATE_RANK = 16
GATE_TAU = 16.0
POOL_WIDTH = D_MODEL // 2
POOL_WINDOWS = (2, 4, 8, 16)
POOL_GROUPS = len(POOL_WINDOWS)
POOL_GROUP = POOL_WIDTH // POOL_GROUPS
POOL_BUF = max(POOL_WINDOWS) - 1
POOL_CARRY = POOL_BUF + 1
PAST_LEN = 1024
EPS = 1e-6

LANES = 128
SUBLANES = 8
GATE_PAD = LANES

C_Q = 0
C_K = C_Q + GLA_DK
C_V = C_K + GLA_DK
C_R = C_V + GLA_DV
W_GLA_COLS = C_R + GLA_DV
C_P = 0
C_GA = C_P + POOL_WIDTH
C_GB = C_GA + D_MODEL

VMEM_LIMIT_BYTES = 56 * 1024 * 1024

PROMPT_TILE = 256
TILES_PER_STEP = 2
META_TILE = 128
FFN_TILE = 512
FFN_PIECES = 2

F32 = jnp.float32
BF16 = jnp.bfloat16


def _level_table(T, block):
    J = int(math.log2(block))
    assert 1 << J == block and block >= 2 * SUBLANES and T % block == 0
    t = np.arange(T)[:, None]
    u = np.arange(T)[None, :]
    hb = np.floor(np.log2(np.maximum(t ^ u, 1))).astype(np.int32)
    lvl = np.where((u > t) | (t // block != u // block), -1, np.where(u == t, J, hb)).astype(np.int32)
    return J, lvl


def _rms(x, g):
    return x * jax.lax.rsqrt(jnp.mean(x * x, axis=-1, keepdims=True) + EPS) * g


def _dot(a, b):
    return jnp.dot(a, b, preferred_element_type=F32)


def _dot_nt(a, b):
    return jax.lax.dot_general(a, b, (((1,), (1,)), ((), ())), preferred_element_type=F32)


def _dot_tn(a, b):
    return jax.lax.dot_general(a, b, (((0,), (0,)), ((), ())), preferred_element_type=F32)


def _log_sigmoid(z):
    return jnp.minimum(z, 0.0) - jnp.log(1.0 + jnp.exp(-jnp.abs(z)))


def _decay_factors(g, e_scr, ep_scr, es_scr, T, J):
    a = jnp.exp(g).reshape(T // SUBLANES, SUBLANES, GLA_DK)
    r = jax.lax.broadcasted_iota(jnp.int32, a.shape, 1)
    one = jnp.ones_like(a)

    def prev(x, s):
        return pltpu.roll(x, s, axis=1)

    def nxt(x, s):
        return pltpu.roll(x, SUBLANES - s, axis=1)

    s1 = a * jnp.where(r >= 1, prev(a, 1), one)
    s2 = s1 * jnp.where(r >= 2, prev(s1, 2), one)
    ep = s2 * jnp.where(r >= 4, prev(s2, 4), one)
    c = jnp.where(r <= 6, nxt(a, 1), one)
    d1 = c * jnp.where(r <= 6, nxt(c, 1), one)
    d2 = d1 * jnp.where(r <= 5, nxt(d1, 2), one)
    es = d2 * jnp.where(r <= 3, nxt(d2, 4), one)

    r4 = r & 3
    e0 = jnp.where((r & 1) == 1, a, one)
    e1 = jnp.where(r4 == 0, c, jnp.where(r4 == 1, one, jnp.where(r4 == 2, a, s1)))
    lower2 = jnp.where(r == 0, d1 * nxt(c, 2), jnp.where(r == 1, d1, jnp.where(r == 2, c, one)))
    upper2 = jnp.where(r == 4, a, jnp.where(r == 5, s1, jnp.where(r == 6, s1 * prev(a, 2), s2)))
    e2 = jnp.where(r <= 3, lower2, upper2)
    for j, e in enumerate((e0, e1, e2)):
        e_scr[j] = e.reshape(T, GLA_DK)
    ep_scr[...] = ep.reshape(T, GLA_DK)
    es_scr[...] = es.reshape(T, GLA_DK)

    for j in range(3, J):
        m = 1 << j
        for lo in range(0, T, 2 * m):
            mid, hi = lo + m, lo + 2 * m
            e_scr[j, lo:mid] = es_scr[lo:mid]
            e_scr[j, mid:hi] = ep_scr[mid:hi]
            lower_total = ep_scr[mid - 1:mid, :]
            upper_total = ep_scr[hi - 1:hi, :]
            ep_scr[mid:hi] = ep_scr[mid:hi] * lower_total
            es_scr[lo:mid] = es_scr[lo:mid] * upper_total


def _pool_mix(p, carry, mixed_scr, T):
    n = (T + POOL_CARRY) // SUBLANES
    s = jnp.concatenate([carry, p], axis=0).reshape(n, SUBLANES, POOL_WIDTH)
    tok = s[POOL_CARRY // SUBLANES:]
    r = jax.lax.broadcasted_iota(jnp.int32, (n, SUBLANES, LANES), 1)

    def shifted(x, k):
        down = lambda y: jnp.concatenate([y[:1], y[:-1]], axis=0)
        if k == SUBLANES:
            return down(x)
        xr = pltpu.roll(x, k, axis=1)
        rr = jnp.concatenate([r] * (x.shape[-1] // LANES), axis=-1)
        return jnp.where(rr >= k, xr, down(xr))

    width = 1
    for gi, w in enumerate(POOL_WINDOWS):
        assert w == 2 * width and w <= POOL_CARRY
        s = s + shifted(s, width)
        width = w
        mixed = s[POOL_CARRY // SUBLANES:, :, :POOL_GROUP] * (1.0 / w) - tok[:, :, gi * POOL_GROUP:(gi + 1) * POOL_GROUP]
        mixed_scr[:, gi * POOL_GROUP:(gi + 1) * POOL_GROUP] = mixed.reshape(T, POOL_GROUP)
        s = s[:, :, POOL_GROUP:]


def _tile_front(x, n1_ref, wgla_ref, wgate_ref, wpg_ref, wa2_ref, ba_ref, buf_ref,
                e_scr, ep_scr, es_scr, q_scr, k_scr, v_scr, r_scr, ga_scr, gb_scr, mixed_scr, merged_scr, T, J, S):
    xn = _rms(x, n1_ref[...]).astype(BF16)

    def proj(w_ref, lo, width):
        return _dot(xn, w_ref[:, lo:lo + width])

    a_lr = proj(wgate_ref, 0, GATE_PAD).astype(BF16)
    g = _log_sigmoid(_dot(a_lr, wa2_ref[...]) + ba_ref[...]) * (1.0 / GATE_TAU)

    q_scr[...] = proj(wgla_ref, C_Q, GLA_DK)
    k_scr[...] = proj(wgla_ref, C_K, GLA_DK)
    p = proj(wpg_ref, C_P, POOL_WIDTH)
    v_scr[...] = proj(wgla_ref, C_V, GLA_DV)
    r_scr[...] = proj(wgla_ref, C_R, GLA_DV)
    ga_scr[...] = proj(wpg_ref, C_GA, D_MODEL)
    gb_scr[...] = proj(wpg_ref, C_GB, D_MODEL)

    _decay_factors(g, e_scr, ep_scr, es_scr, T, J)
    Ts = T // S
    for i in range(S):
        _pool_mix(p[i * Ts:(i + 1) * Ts], buf_ref[i], mixed_scr.at[i * Ts:(i + 1) * Ts], Ts)
        buf_ref[i] = p[(i + 1) * Ts - POOL_CARRY:(i + 1) * Ts, :]


def _tile_back(x, gng_ref, wpool_ref, pscale_ref, wout_ref, lvl_ref, s_ref,
               e_scr, ep_scr, es_scr, q_scr, k_scr, v_scr, r_scr, ga_scr, gb_scr, mixed_scr, merged_scr, T, J, S):
    Ts = T // S
    streams = [slice(i * Ts, (i + 1) * Ts) for i in range(S)]
    lvl = lvl_ref[...]
    for h in range(GLA_HEADS):
        ks = slice(h * DK_HEAD, (h + 1) * DK_HEAD)
        vs = slice(h * DV_HEAD, (h + 1) * DV_HEAD)
        qh = q_scr[:, ks] * (DK_HEAD ** -0.5)
        kh = k_scr[:, ks]
        vh = v_scr[:, vs].astype(BF16)

        att = jnp.where(lvl == J, _dot_nt(qh.astype(BF16), kh.astype(BF16)), 0.0)
        for j in range(J):
            e = e_scr[j, :, ks]
            pj = _dot_nt((qh * e).astype(BF16), (kh * e).astype(BF16))
            att = jnp.where(lvl == j, pj, att)

        q_dec = (qh * ep_scr[:, ks]).astype(BF16)
        o = jnp.concatenate([_dot(q_dec[rows], s_ref[i, h].astype(BF16)) for i, rows in enumerate(streams)], axis=0)
        o = o + _dot(att.astype(BF16), vh)

        k_dec = (kh * es_scr[:, ks]).astype(BF16)
        for i, rows in enumerate(streams):
            last = rows.stop - 1
            decay = jnp.broadcast_to(ep_scr[last:last + 1, ks], (DK_HEAD, DK_HEAD)).T
            decay = jnp.concatenate([decay] * (DV_HEAD // DK_HEAD), axis=1)
            s_ref[i, h] = decay * s_ref[i, h] + _dot_tn(k_dec[rows], vh[rows])

        o = o * jax.lax.rsqrt(jnp.mean(o * o, axis=-1, keepdims=True) + EPS) * gng_ref[:, vs]
        y_b = _dot(mixed_scr[:, ks].astype(BF16), wpool_ref[h])
        r = r_scr[:, vs]
        y_a = r * jax.nn.sigmoid(r) * o
        merged = jax.nn.sigmoid(ga_scr[:, vs]) * y_a + jax.nn.sigmoid(gb_scr[:, vs]) * (y_b * pscale_ref[:, vs])
        merged_scr[:, vs] = merged.astype(BF16)

    return x + _dot(merged_scr[...], wout_ref[...])


def _mixer_kernel(x_ref, s0_ref, buf0_ref, n1_ref, wgla_ref, wgate_ref, wpg_ref, wa2_ref, ba_ref, gng_ref,
                  wpool_ref, pscale_ref, wout_ref, lvl_ref,
                  h_ref, s_ref, buf_ref, *scratch, T, J, U, S):
    @pl.when(pl.program_id(1) == 0)
    def _():
        s_ref[...] = s0_ref[...]
        buf_ref[...] = buf0_ref[...]

    tiles = [[scr.at[u] for scr in scratch] for u in range(U)]
    Ts = T // S
    tokens = [slice(u * Ts, (u + 1) * Ts) for u in range(U)]
    for u in range(U):
        _tile_front(x_ref[:, tokens[u], :].reshape(T, D_MODEL), n1_ref, wgla_ref, wgate_ref, wpg_ref, wa2_ref,
                    ba_ref, buf_ref, *tiles[u], T, J, S)
    for u in range(U):
        out = _tile_back(x_ref[:, tokens[u], :].reshape(T, D_MODEL), gng_ref, wpool_ref, pscale_ref, wout_ref,
                         lvl_ref, s_ref, *tiles[u], T, J, S)
        h_ref[:, tokens[u], :] = out.reshape(S, Ts, D_MODEL)


def _const_spec(shape):
    zeros = (0,) * len(shape)
    return pl.BlockSpec(shape, lambda b, t: zeros, pipeline_mode=pl.Buffered(1))


def _mixer(x, s0, buf0, w, Ts, S, shared_state):
    B, L, D = x.shape
    assert L % Ts == 0 and Ts >= POOL_CARRY and B % S == 0 and (S == 1 or (L == Ts and not shared_state))
    T = S * Ts
    U = math.gcd(L // Ts, TILES_PER_STEP)
    J, lvl = _level_table(T, Ts)
    consts = (jnp.asarray(lvl),)

    state_idx = (lambda b, t: (0, 0, 0, 0)) if shared_state else (lambda b, t: (b, 0, 0, 0))
    buf_idx = (lambda b, t: (0, 0, 0)) if shared_state else (lambda b, t: (b, 0, 0))
    weights = (w["norm1_g"], w["w_in"], w["w_gate"], w["w_poolgate"], w["w_alpha2"], w["b_alpha"], w["gla_norm_g"],
               w["w_pool"], w["pool_scale"], w["w_out"]) + consts
    weight_blocks = [a.shape for a in weights]
    weight_blocks[1] = (D_MODEL, W_GLA_COLS)
    in_specs = [
        pl.BlockSpec((S, U * Ts, D), lambda b, t: (b, t, 0)),
        pl.BlockSpec((S, GLA_HEADS, DK_HEAD, DV_HEAD), state_idx),
        pl.BlockSpec((S, POOL_CARRY, POOL_WIDTH), buf_idx),
    ] + [_const_spec(shape) for shape in weight_blocks]
    out_shape = (
        jax.ShapeDtypeStruct((B, L, D), F32),
        jax.ShapeDtypeStruct((B, GLA_HEADS, DK_HEAD, DV_HEAD), F32),
        jax.ShapeDtypeStruct((B, POOL_CARRY, POOL_WIDTH), F32),
    )
    out_specs = (
        pl.BlockSpec((S, U * Ts, D), lambda b, t: (b, t, 0)),
        pl.BlockSpec((S, GLA_HEADS, DK_HEAD, DV_HEAD), lambda b, t: (b, 0, 0, 0)),
        pl.BlockSpec((S, POOL_CARRY, POOL_WIDTH), lambda b, t: (b, 0, 0)),
    )
    scratch = [
        pltpu.VMEM((U, J, T, GLA_DK), F32),
        pltpu.VMEM((U, T, GLA_DK), F32),
        pltpu.VMEM((U, T, GLA_DK), F32),
        pltpu.VMEM((U, T, GLA_DK), F32),
        pltpu.VMEM((U, T, GLA_DK), F32),
        pltpu.VMEM((U, T, GLA_DV), F32),
        pltpu.VMEM((U, T, GLA_DV), F32),
        pltpu.VMEM((U, T, D_MODEL), F32),
        pltpu.VMEM((U, T, D_MODEL), F32),
        pltpu.VMEM((U, T, POOL_WIDTH), F32),
        pltpu.VMEM((U, T, D_MODEL), BF16),
    ]
    return pl.pallas_call(
        functools.partial(_mixer_kernel, T=T, J=J, U=U, S=S),
        out_shape=out_shape,
        grid=(B // S, L // (U * Ts)),
        in_specs=in_specs,
        out_specs=out_specs,
        scratch_shapes=scratch,
        compiler_params=pltpu.CompilerParams(
            dimension_semantics=("arbitrary", "arbitrary"),
            vmem_limit_bytes=VMEM_LIMIT_BYTES),
        name=f"mixer_s{S}t{Ts}x{U}",
    )(x, s0, buf0, *weights)


def _ffn_kernel(h_ref, n2_ref, wg_ref, wu_ref, wd_ref, nf_ref, y_ref):
    R = h_ref.shape[0] // FFN_PIECES
    rows = [slice(i * R, (i + 1) * R) for i in range(FFN_PIECES)]
    hs = [h_ref[r, :] for r in rows]
    hns = [_rms(h, n2_ref[...]).astype(BF16) for h in hs]
    for i in range(FFN_PIECES):
        gate = _dot(hns[i], wg_ref[...])
        up = _dot(hns[i], wu_ref[...])
        act = (gate * jax.nn.sigmoid(gate) * up).astype(BF16)
        h2 = hs[i] + _dot(act, wd_ref[...])
        y_ref[rows[i], :] = _rms(h2, nf_ref[...])


def _ffn(h, w, R):
    N, D = h.shape
    assert N % R == 0
    weights = (w["norm2_g"], w["w_ffn_gate"], w["w_ffn_up"], w["w_ffn_down"], w["norm_f_g"])

    def const(a):
        zeros = (0,) * a.ndim
        return pl.BlockSpec(a.shape, lambda i: zeros, pipeline_mode=pl.Buffered(1))

    return pl.pallas_call(
        _ffn_kernel,
        out_shape=jax.ShapeDtypeStruct((N, D), F32),
        grid=(N // R,),
        in_specs=[pl.BlockSpec((R, D), lambda i: (i, 0))] + [const(a) for a in weights],
        out_specs=pl.BlockSpec((R, D), lambda i: (i, 0)),
        compiler_params=pltpu.CompilerParams(
            dimension_semantics=("arbitrary",),
            vmem_limit_bytes=VMEM_LIMIT_BYTES),
        name=f"ffn_r{R}",
    )(h, *weights)


def _pack_weights(norm1_g, w_in, w_alpha2, b_alpha, gla_norm_g, w_pool, pool_scale, w_out,
                  norm2_g, w_ffn_gate, w_ffn_up, w_ffn_down, norm_f_g):
    o_p = W_GLA_COLS + GATE_RANK
    w_in_bf = w_in.astype(BF16)
    w_gate = jnp.pad(w_in_bf[:, W_GLA_COLS:o_p], ((0, 0), (0, GATE_PAD - GATE_RANK)))
    w_poolgate = w_in_bf[:, o_p:]
    w_alpha2_packed = jnp.pad(w_alpha2, ((0, GATE_PAD - GATE_RANK), (0, 0))).astype(BF16)
    row = lambda v: v.reshape(1, -1).astype(F32)
    return dict(
        norm1_g=row(norm1_g), w_in=w_in_bf, w_gate=w_gate, w_poolgate=w_poolgate,
        w_alpha2=w_alpha2_packed, b_alpha=row(b_alpha),
        gla_norm_g=row(gla_norm_g), w_pool=w_pool.astype(BF16), pool_scale=row(pool_scale),
        w_out=w_out.astype(BF16), norm2_g=row(norm2_g), w_ffn_gate=w_ffn_gate.astype(BF16),
        w_ffn_up=w_ffn_up.astype(BF16), w_ffn_down=w_ffn_down.astype(BF16), norm_f_g=row(norm_f_g))


def _largest_tile(n, cap):
    t = cap
    while n % t:
        t //= 2
    return t


def kernel(x_prompt, x_sample, state_gla, state_pool, meta_tokens, norm1_g, w_in, w_alpha2, b_alpha,
           gla_norm_g, w_pool, pool_scale, w_out, norm2_g, w_ffn_gate, w_ffn_up, w_ffn_down, norm_f_g):
    w = _pack_weights(norm1_g, w_in, w_alpha2, b_alpha, gla_norm_g, w_pool, pool_scale, w_out,
                      norm2_g, w_ffn_gate, w_ffn_up, w_ffn_down, norm_f_g)
    B, L, D = x_prompt.shape
    Bs, Ls, _ = x_sample.shape
    n_meta = meta_tokens.shape[0]
    assert POOL_BUF <= n_meta <= META_TILE and PAST_LEN >= POOL_BUF

    s_zero = jnp.zeros((1, GLA_HEADS, DK_HEAD, DV_HEAD), F32)
    buf_zero = jnp.zeros((1, POOL_CARRY, POOL_WIDTH), F32)
    x_meta = jnp.pad(meta_tokens.astype(F32), ((META_TILE - n_meta, 0), (0, 0)))[None]
    _, s_meta, buf_meta = _mixer(x_meta, s_zero, buf_zero, w, META_TILE, 1, True)

    h_p, s_p, buf_p = _mixer(x_prompt, s_meta, buf_meta, w, _largest_tile(L, PROMPT_TILE), 1, True)

    buf_s0 = jnp.pad(state_pool.astype(F32), ((0, 0), (POOL_CARRY - POOL_BUF, 0), (0, 0)))
    ts = _largest_tile(Ls, PROMPT_TILE)
    streams = math.gcd(Bs, PROMPT_TILE // ts) if ts == Ls else 1
    h_s, s_s, buf_s = _mixer(x_sample, state_gla.astype(F32), buf_s0, w, ts, streams, False)

    y_p = _ffn(h_p.reshape(B * L, D), w, _largest_tile(B * L, FFN_TILE)).reshape(B, L, D)
    y_s = _ffn(h_s.reshape(Bs * Ls, D), w, _largest_tile(Bs * Ls, FFN_TILE)).reshape(Bs, Ls, D)
    return (y_p, y_s,
            s_p.astype(state_gla.dtype), buf_p[:, POOL_CARRY - POOL_BUF:].astype(state_pool.dtype),
            s_s.astype(state_gla.dtype), buf_s[:, POOL_CARRY - POOL_BUF:].astype(state_pool.dtype))
```

```python
import functools
import math

import numpy as np
import jax
import jax.numpy as jnp
from jax.experimental import pallas as pl
from jax.experimental.pallas import tpu as pltpu

D_MODEL = 1024
GLA_HEADS = 4
GLA_DK = D_MODEL // 2
GLA_DV = D_MODEL
DK_HEAD = GLA_DK // GLA_HEADS
DV_HEAD = GLA_DV // GLA_HEADS
GATE_RANK = 16
GATE_TAU = 16.0
POOL_WIDTH = D_MODEL // 2
POOL_WINDOWS = (2, 4, 8, 16)
POOL_GROUPS = len(POOL_WINDOWS)
POOL_GROUP = POOL_WIDTH // POOL_GROUPS
POOL_BUF = max(POOL_WINDOWS) - 1
POOL_CARRY = POOL_BUF + 1
PAST_LEN = 1024
EPS = 1e-6

LANES = 128
SUBLANES = 8
GATE_PAD = LANES

C_Q = 0
C_K = C_Q + GLA_DK
C_V = C_K + GLA_DK
C_R = C_V + GLA_DV
W_GLA_COLS = C_R + GLA_DV
C_P = 0
C_GA = C_P + POOL_WIDTH
C_GB = C_GA + D_MODEL

VMEM_LIMIT_BYTES = 56 * 1024 * 1024

PROMPT_TILE = 256
TILES_PER_STEP = 2
META_TILE = 128
FFN_TILE = 512
FFN_PIECES = 2

F32 = jnp.float32
BF16 = jnp.bfloat16


def _level_table(T, block):
    J = int(math.log2(block))
    assert 1 << J == block and block >= 2 * SUBLANES and T % block == 0
    t = np.arange(T)[:, None]
    u = np.arange(T)[None, :]
    hb = np.floor(np.log2(np.maximum(t ^ u, 1))).astype(np.int32)
    lvl = np.where((u > t) | (t // block != u // block), -1, np.where(u == t, J, hb)).astype(np.int32)
    return J, lvl


def _rms(x, g):
    return x * jax.lax.rsqrt(jnp.mean(x * x, axis=-1, keepdims=True) + EPS) * g


def _dot(a, b):
    return jnp.dot(a, b, preferred_element_type=F32)


def _dot_nt(a, b):
    return jax.lax.dot_general(a, b, (((1,), (1,)), ((), ())), preferred_element_type=F32)


def _dot_tn(a, b):
    return jax.lax.dot_general(a, b, (((0,), (0,)), ((), ())), preferred_element_type=F32)


def _log_sigmoid(z):
    return jnp.minimum(z, 0.0) - jnp.log(1.0 + jnp.exp(-jnp.abs(z)))


def _decay_factors(g, e_scr, ep_scr, es_scr, T, J):
    a = jnp.exp(g).reshape(T // SUBLANES, SUBLANES, GLA_DK)
    r = jax.lax.broadcasted_iota(jnp.int32, a.shape, 1)
    one = jnp.ones_like(a)

    def prev(x, s):
        return pltpu.roll(x, s, axis=1)

    def nxt(x, s):
        return pltpu.roll(x, SUBLANES - s, axis=1)

    s1 = a * jnp.where(r >= 1, prev(a, 1), one)
    s2 = s1 * jnp.where(r >= 2, prev(s1, 2), one)
    ep = s2 * jnp.where(r >= 4, prev(s2, 4), one)
    c = jnp.where(r <= 6, nxt(a, 1), one)
    d1 = c * jnp.where(r <= 6, nxt(c, 1), one)
    d2 = d1 * jnp.where(r <= 5, nxt(d1, 2), one)
    es = d2 * jnp.where(r <= 3, nxt(d2, 4), one)

    r4 = r & 3
    e0 = jnp.where((r & 1) == 1, a, one)
    e1 = jnp.where(r4 == 0, c, jnp.where(r4 == 1, one, jnp.where(r4 == 2, a, s1)))
    lower2 = jnp.where(r == 0, d1 * nxt(c, 2), jnp.where(r == 1, d1, jnp.where(r == 2, c, one)))
    upper2 = jnp.where(r == 4, a, jnp.where(r == 5, s1, jnp.where(r == 6, s1 * prev(a, 2), s2)))
    e2 = jnp.where(r <= 3, lower2, upper2)
    for j, e in enumerate((e0, e1, e2)):
        e_scr[j] = e.reshape(T, GLA_DK)
    ep_scr[...] = ep.reshape(T, GLA_DK)
    es_scr[...] = es.reshape(T, GLA_DK)

    for j in range(3, J):
        m = 1 << j
        for lo in range(0, T, 2 * m):
            mid, hi = lo + m, lo + 2 * m
            e_scr[j, lo:mid] = es_scr[lo:mid]
            e_scr[j, mid:hi] = ep_scr[mid:hi]
            lower_total = ep_scr[mid - 1:mid, :]
            upper_total = ep_scr[hi - 1:hi, :]
            ep_scr[mid:hi] = ep_scr[mid:hi] * lower_total
            es_scr[lo:mid] = es_scr[lo:mid] * upper_total


def _pool_mix(p, carry, mixed_scr, T):
    n = (T + POOL_CARRY) // SUBLANES
    s = jnp.concatenate([carry, p], axis=0).reshape(n, SUBLANES, POOL_WIDTH)
    tok = s[POOL_CARRY // SUBLANES:]
    r = jax.lax.broadcasted_iota(jnp.int32, (n, SUBLANES, LANES), 1)

    def shifted(x, k):
        down = lambda y: jnp.concatenate([y[:1], y[:-1]], axis=0)
        if k == SUBLANES:
            return down(x)
        xr = pltpu.roll(x, k, axis=1)
        rr = jnp.concatenate([r] * (x.shape[-1] // LANES), axis=-1)
        return jnp.where(rr >= k, xr, down(xr))

    width = 1
    for gi, w in enumerate(POOL_WINDOWS):
        assert w == 2 * width and w <= POOL_CARRY
        s = s + shifted(s, width)
        width = w
        mixed = s[POOL_CARRY // SUBLANES:, :, :POOL_GROUP] * (1.0 / w) - tok[:, :, gi * POOL_GROUP:(gi + 1) * POOL_GROUP]
        mixed_scr[:, gi * POOL_GROUP:(gi + 1) * POOL_GROUP] = mixed.reshape(T, POOL_GROUP)
        s = s[:, :, POOL_GROUP:]


def _tile_front(x, n1_ref, wgla_ref, wgate_ref, wpg_ref, wa2_ref, ba_ref, buf_ref,
                e_scr, ep_scr, es_scr, q_scr, k_scr, v_scr, r_scr, ga_scr, gb_scr, mixed_scr, merged_scr, T, J, S):
    xn = _rms(x, n1_ref[...]).astype(BF16)

    def proj(w_ref, lo, width):
        return _dot(xn, w_ref[:, lo:lo + width])

    a_lr = proj(wgate_ref, 0, GATE_PAD).astype(BF16)
    g = _log_sigmoid(_dot(a_lr, wa2_ref[...]) + ba_ref[...]) * (1.0 / GATE_TAU)

    q_scr[...] = proj(wgla_ref, C_Q, GLA_DK)
    k_scr[...] = proj(wgla_ref, C_K, GLA_DK)
    p = proj(wpg_ref, C_P, POOL_WIDTH)
    v_scr[...] = proj(wgla_ref, C_V, GLA_DV)
    r_scr[...] = proj(wgla_ref, C_R, GLA_DV)
    ga_scr[...] = proj(wpg_ref, C_GA, D_MODEL)
    gb_scr[...] = proj(wpg_ref, C_GB, D_MODEL)

    _decay_factors(g, e_scr, ep_scr, es_scr, T, J)
    Ts = T // S
    for i in range(S):
        _pool_mix(p[i * Ts:(i + 1) * Ts], buf_ref[i], mixed_scr.at[i * Ts:(i + 1) * Ts], Ts)
        buf_ref[i] = p[(i + 1) * Ts - POOL_CARRY:(i + 1) * Ts, :]


def _tile_attention(lvl_ref, e_scr, q_scr, k_scr, J):
    lvl = lvl_ref[...]
    atts = []
    for h in range(GLA_HEADS):
        ks = slice(h * DK_HEAD, (h + 1) * DK_HEAD)
        qh = q_scr[:, ks] * (DK_HEAD ** -0.5)
        kh = k_scr[:, ks]
        att = jnp.where(lvl == J, _dot_nt(qh.astype(BF16), kh.astype(BF16)), 0.0)
        for j in range(J):
            e = e_scr[j, :, ks]
            pj = _dot_nt((qh * e).astype(BF16), (kh * e).astype(BF16))
            att = jnp.where(lvl == j, pj, att)
        atts.append(att.astype(BF16))
    return atts


def _tile_back(x, atts, gng_ref, wpool_ref, pscale_ref, wout_ref, s_ref,
               e_scr, ep_scr, es_scr, q_scr, k_scr, v_scr, r_scr, ga_scr, gb_scr, mixed_scr, merged_scr, T, J, S):
    Ts = T // S
    streams = [slice(i * Ts, (i + 1) * Ts) for i in range(S)]
    for h in range(GLA_HEADS):
        ks = slice(h * DK_HEAD, (h + 1) * DK_HEAD)
        vs = slice(h * DV_HEAD, (h + 1) * DV_HEAD)
        qh = q_scr[:, ks] * (DK_HEAD ** -0.5)
        kh = k_scr[:, ks]
        vh = v_scr[:, vs].astype(BF16)

        q_dec = (qh * ep_scr[:, ks]).astype(BF16)
        o = jnp.concatenate([_dot(q_dec[rows], s_ref[i, h].astype(BF16)) for i, rows in enumerate(streams)], axis=0)
        o = o + _dot(atts[h], vh)

        k_dec = (kh * es_scr[:, ks]).astype(BF16)
        for i, rows in enumerate(streams):
            last = rows.stop - 1
            decay = jnp.broadcast_to(ep_scr[last:last + 1, ks], (DK_HEAD, DK_HEAD)).T
            decay = jnp.concatenate([decay] * (DV_HEAD // DK_HEAD), axis=1)
            s_ref[i, h] = decay * s_ref[i, h] + _dot_tn(k_dec[rows], vh[rows])

        o = o * jax.lax.rsqrt(jnp.mean(o * o, axis=-1, keepdims=True) + EPS) * gng_ref[:, vs]
        y_b = _dot(mixed_scr[:, ks].astype(BF16), wpool_ref[h])
        r = r_scr[:, vs]
        y_a = r * jax.nn.sigmoid(r) * o
        merged = jax.nn.sigmoid(ga_scr[:, vs]) * y_a + jax.nn.sigmoid(gb_scr[:, vs]) * (y_b * pscale_ref[:, vs])
        merged_scr[:, vs] = merged.astype(BF16)

    return x + _dot(merged_scr[...], wout_ref[...])


def _mixer_kernel(x_ref, s0_ref, buf0_ref, n1_ref, wgla_ref, wgate_ref, wpg_ref, wa2_ref, ba_ref, gng_ref,
                  wpool_ref, pscale_ref, wout_ref, lvl_ref,
                  h_ref, s_ref, buf_ref, *scratch, T, J, U, S):
    @pl.when(pl.program_id(1) == 0)
    def _():
        s_ref[...] = s0_ref[...]
        buf_ref[...] = buf0_ref[...]

    tiles = [[scr.at[u] for scr in scratch] for u in range(U)]
    Ts = T // S
    tokens = [slice(u * Ts, (u + 1) * Ts) for u in range(U)]
    for u in range(U):
        _tile_front(x_ref[:, tokens[u], :].reshape(T, D_MODEL), n1_ref, wgla_ref, wgate_ref, wpg_ref, wa2_ref,
                    ba_ref, buf_ref, *tiles[u], T, J, S)
    atts = [_tile_attention(lvl_ref, tiles[u][0], tiles[u][3], tiles[u][4], J) for u in range(U)]
    for u in range(U):
        out = _tile_back(x_ref[:, tokens[u], :].reshape(T, D_MODEL), atts[u], gng_ref, wpool_ref, pscale_ref,
                         wout_ref, s_ref, *tiles[u], T, J, S)
        h_ref[:, tokens[u], :] = out.reshape(S, Ts, D_MODEL)


def _const_spec(shape):
    zeros = (0,) * len(shape)
    return pl.BlockSpec(shape, lambda b, t: zeros, pipeline_mode=pl.Buffered(1))


def _mixer(x, s0, buf0, w, Ts, S, shared_state):
    B, L, D = x.shape
    assert L % Ts == 0 and Ts >= POOL_CARRY and B % S == 0 and (S == 1 or (L == Ts and not shared_state))
    T = S * Ts
    U = math.gcd(L // Ts, TILES_PER_STEP)
    J, lvl = _level_table(T, Ts)
    consts = (jnp.asarray(lvl),)

    state_idx = (lambda b, t: (0, 0, 0, 0)) if shared_state else (lambda b, t: (b, 0, 0, 0))
    buf_idx = (lambda b, t: (0, 0, 0)) if shared_state else (lambda b, t: (b, 0, 0))
    weights = (w["norm1_g"], w["w_in"], w["w_gate"], w["w_poolgate"], w["w_alpha2"], w["b_alpha"], w["gla_norm_g"],
               w["w_pool"], w["pool_scale"], w["w_out"]) + consts
    weight_blocks = [a.shape for a in weights]
    weight_blocks[1] = (D_MODEL, W_GLA_COLS)
    in_specs = [
        pl.BlockSpec((S, U * Ts, D), lambda b, t: (b, t, 0)),
        pl.BlockSpec((S, GLA_HEADS, DK_HEAD, DV_HEAD), state_idx),
        pl.BlockSpec((S, POOL_CARRY, POOL_WIDTH), buf_idx),
    ] + [_const_spec(shape) for shape in weight_blocks]
    out_shape = (
        jax.ShapeDtypeStruct((B, L, D), F32),
        jax.ShapeDtypeStruct((B, GLA_HEADS, DK_HEAD, DV_HEAD), F32),
        jax.ShapeDtypeStruct((B, POOL_CARRY, POOL_WIDTH), F32),
    )
    out_specs = (
        pl.BlockSpec((S, U * Ts, D), lambda b, t: (b, t, 0)),
        pl.BlockSpec((S, GLA_HEADS, DK_HEAD, DV_HEAD), lambda b, t: (b, 0, 0, 0)),
        pl.BlockSpec((S, POOL_CARRY, POOL_WIDTH), lambda b, t: (b, 0, 0)),
    )
    scratch = [
        pltpu.VMEM((U, J, T, GLA_DK), F32),
        pltpu.VMEM((U, T, GLA_DK), F32),
        pltpu.VMEM((U, T, GLA_DK), F32),
        pltpu.VMEM((U, T, GLA_DK), F32),
        pltpu.VMEM((U, T, GLA_DK), F32),
        pltpu.VMEM((U, T, GLA_DV), F32),
        pltpu.VMEM((U, T, GLA_DV), F32),
        pltpu.VMEM((U, T, D_MODEL), F32),
        pltpu.VMEM((U, T, D_MODEL), F32),
        pltpu.VMEM((U, T, POOL_WIDTH), F32),
        pltpu.VMEM((U, T, D_MODEL), BF16),
    ]
    return pl.pallas_call(
        functools.partial(_mixer_kernel, T=T, J=J, U=U, S=S),
        out_shape=out_shape,
        grid=(B // S, L // (U * Ts)),
        in_specs=in_specs,
        out_specs=out_specs,
        scratch_shapes=scratch,
        compiler_params=pltpu.CompilerParams(
            dimension_semantics=("arbitrary", "arbitrary"),
            vmem_limit_bytes=VMEM_LIMIT_BYTES),
        name=f"mixer_s{S}t{Ts}x{U}",
    )(x, s0, buf0, *weights)


def _ffn_kernel(h_ref, n2_ref, wg_ref, wu_ref, wd_ref, nf_ref, y_ref):
    R = h_ref.shape[0] // FFN_PIECES
    rows = [slice(i * R, (i + 1) * R) for i in range(FFN_PIECES)]
    hs = [h_ref[r, :] for r in rows]
    hns = [_rms(h, n2_ref[...]).astype(BF16) for h in hs]
    for i in range(FFN_PIECES):
        gate = _dot(hns[i], wg_ref[...])
        up = _dot(hns[i], wu_ref[...])
        act = (gate * jax.nn.sigmoid(gate) * up).astype(BF16)
        h2 = hs[i] + _dot(act, wd_ref[...])
        y_ref[rows[i], :] = _rms(h2, nf_ref[...])


def _ffn(h, w, R):
    N, D = h.shape
    assert N % R == 0
    weights = (w["norm2_g"], w["w_ffn_gate"], w["w_ffn_up"], w["w_ffn_down"], w["norm_f_g"])

    def const(a):
        zeros = (0,) * a.ndim
        return pl.BlockSpec(a.shape, lambda i: zeros, pipeline_mode=pl.Buffered(1))

    return pl.pallas_call(
        _ffn_kernel,
        out_shape=jax.ShapeDtypeStruct((N, D), F32),
        grid=(N // R,),
        in_specs=[pl.BlockSpec((R, D), lambda i: (i, 0))] + [const(a) for a in weights],
        out_specs=pl.BlockSpec((R, D), lambda i: (i, 0)),
        compiler_params=pltpu.CompilerParams(
            dimension_semantics=("arbitrary",),
            vmem_limit_bytes=VMEM_LIMIT_BYTES),
        name=f"ffn_r{R}",
    )(h, *weights)


def _pack_weights(norm1_g, w_in, w_alpha2, b_alpha, gla_norm_g, w_pool, pool_scale, w_out,
                  norm2_g, w_ffn_gate, w_ffn_up, w_ffn_down, norm_f_g):
    o_p = W_GLA_COLS + GATE_RANK
    w_in_bf = w_in.astype(BF16)
    w_gate = jnp.pad(w_in_bf[:, W_GLA_COLS:o_p], ((0, 0), (0, GATE_PAD - GATE_RANK)))
    w_poolgate = w_in_bf[:, o_p:]
    w_alpha2_packed = jnp.pad(w_alpha2, ((0, GATE_PAD - GATE_RANK), (0, 0))).astype(BF16)
    row = lambda v: v.reshape(1, -1).astype(F32)
    return dict(
        norm1_g=row(norm1_g), w_in=w_in_bf, w_gate=w_gate, w_poolgate=w_poolgate,
        w_alpha2=w_alpha2_packed, b_alpha=row(b_alpha),
        gla_norm_g=row(gla_norm_g), w_pool=w_pool.astype(BF16), pool_scale=row(pool_scale),
        w_out=w_out.astype(BF16), norm2_g=row(norm2_g), w_ffn_gate=w_ffn_gate.astype(BF16),
        w_ffn_up=w_ffn_up.astype(BF16), w_ffn_down=w_ffn_down.astype(BF16), norm_f_g=row(norm_f_g))


def _largest_tile(n, cap):
    t = cap
    while n % t:
        t //= 2
    return t


def kernel(x_prompt, x_sample, state_gla, state_pool, meta_tokens, norm1_g, w_in, w_alpha2, b_alpha,
           gla_norm_g, w_pool, pool_scale, w_out, norm2_g, w_ffn_gate, w_ffn_up, w_ffn_down, norm_f_g):
    w = _pack_weights(norm1_g, w_in, w_alpha2, b_alpha, gla_norm_g, w_pool, pool_scale, w_out,
                      norm2_g, w_ffn_gate, w_ffn_up, w_ffn_down, norm_f_g)
    B, L, D = x_prompt.shape
    Bs, Ls, _ = x_sample.shape
    n_meta = meta_tokens.shape[0]
    assert POOL_BUF <= n_meta <= META_TILE and PAST_LEN >= POOL_BUF

    s_zero = jnp.zeros((1, GLA_HEADS, DK_HEAD, DV_HEAD), F32)
    buf_zero = jnp.zeros((1, POOL_CARRY, POOL_WIDTH), F32)
    x_meta = jnp.pad(meta_tokens.astype(F32), ((META_TILE - n_meta, 0), (0, 0)))[None]
    _, s_meta, buf_meta = _mixer(x_meta, s_zero, buf_zero, w, META_TILE, 1, True)

    h_p, s_p, buf_p = _mixer(x_prompt, s_meta, buf_meta, w, _largest_tile(L, PROMPT_TILE), 1, True)

    buf_s0 = jnp.pad(state_pool.astype(F32), ((0, 0), (POOL_CARRY - POOL_BUF, 0), (0, 0)))
    ts = _largest_tile(Ls, PROMPT_TILE)
    streams = math.gcd(Bs, PROMPT_TILE // ts) if ts == Ls else 1
    h_s, s_s, buf_s = _mixer(x_sample, state_gla.astype(F32), buf_s0, w, ts, streams, False)

    y_p = _ffn(h_p.reshape(B * L, D), w, _largest_tile(B * L, FFN_TILE)).reshape(B, L, D)
    y_s = _ffn(h_s.reshape(Bs * Ls, D), w, _largest_tile(Bs * Ls, FFN_TILE)).reshape(Bs, Ls, D)
    return (y_p, y_s,
            s_p.astype(state_gla.dtype), buf_p[:, POOL_CARRY - POOL_BUF:].astype(state_pool.dtype),
            s_s.astype(state_gla.dtype), buf_s[:, POOL_CARRY - POOL_BUF:].astype(state_pool.dtype))
```

```python
import functools
import math

import numpy as np
import jax
import jax.numpy as jnp
from jax.experimental import pallas as pl
from jax.experimental.pallas import tpu as pltpu

D_MODEL = 1024
GLA_HEADS = 4
GLA_DK = D_MODEL // 2
GLA_DV = D_MODEL
DK_HEAD = GLA_DK // GLA_HEADS
DV_HEAD = GLA_DV // GLA_HEADS
GATE_RANK = 16
GATE_TAU = 16.0
POOL_WIDTH = D_MODEL // 2
POOL_WINDOWS = (2, 4, 8, 16)
POOL_GROUPS = len(POOL_WINDOWS)
POOL_GROUP = POOL_WIDTH // POOL_GROUPS
POOL_BUF = max(POOL_WINDOWS) - 1
POOL_CARRY = POOL_BUF + 1
PAST_LEN = 1024
EPS = 1e-6

LANES = 128
SUBLANES = 8
GATE_PAD = LANES

C_Q = 0
C_K = C_Q + GLA_DK
C_V = C_K + GLA_DK
C_R = C_V + GLA_DV
W_GLA_COLS = C_R + GLA_DV
C_P = 0
C_GA = C_P + POOL_WIDTH
C_GB = C_GA + D_MODEL

VMEM_LIMIT_BYTES = 56 * 1024 * 1024

PROMPT_TILE = 256
TILES_PER_STEP = 2
META_TILE = 128
FFN_TILE = 512
FFN_PIECES = 2

F32 = jnp.float32
BF16 = jnp.bfloat16


def _level_table(T, block):
    J = int(math.log2(block))
    assert 1 << J == block and block >= 2 * SUBLANES and T % block == 0
    t = np.arange(T)[:, None]
    u = np.arange(T)[None, :]
    hb = np.floor(np.log2(np.maximum(t ^ u, 1))).astype(np.int32)
    lvl = np.where((u > t) | (t // block != u // block), -1, np.where(u == t, J, hb)).astype(np.int32)
    return J, lvl


def _rms(x, g):
    return x * jax.lax.rsqrt(jnp.mean(x * x, axis=-1, keepdims=True) + EPS) * g


def _dot(a, b):
    return jnp.dot(a, b, preferred_element_type=F32)


def _dot_nt(a, b):
    return jax.lax.dot_general(a, b, (((1,), (1,)), ((), ())), preferred_element_type=F32)


def _dot_tn(a, b):
    return jax.lax.dot_general(a, b, (((0,), (0,)), ((), ())), preferred_element_type=F32)


def _log_sigmoid(z):
    return jnp.minimum(z, 0.0) - jnp.log(1.0 + jnp.exp(-jnp.abs(z)))


def _decay_factors(g, e_scr, ep_scr, es_scr, T, J):
    a = jnp.exp(g).reshape(T // SUBLANES, SUBLANES, GLA_DK)
    r = jax.lax.broadcasted_iota(jnp.int32, a.shape, 1)
    one = jnp.ones_like(a)

    def prev(x, s):
        return pltpu.roll(x, s, axis=1)

    def nxt(x, s):
        return pltpu.roll(x, SUBLANES - s, axis=1)

    s1 = a * jnp.where(r >= 1, prev(a, 1), one)
    s2 = s1 * jnp.where(r >= 2, prev(s1, 2), one)
    ep = s2 * jnp.where(r >= 4, prev(s2, 4), one)
    c = jnp.where(r <= 6, nxt(a, 1), one)
    d1 = c * jnp.where(r <= 6, nxt(c, 1), one)
    d2 = d1 * jnp.where(r <= 5, nxt(d1, 2), one)
    es = d2 * jnp.where(r <= 3, nxt(d2, 4), one)

    r4 = r & 3
    e0 = jnp.where((r & 1) == 1, a, one)
    e1 = jnp.where(r4 == 0, c, jnp.where(r4 == 1, one, jnp.where(r4 == 2, a, s1)))
    lower2 = jnp.where(r == 0, d1 * nxt(c, 2), jnp.where(r == 1, d1, jnp.where(r == 2, c, one)))
    upper2 = jnp.where(r == 4, a, jnp.where(r == 5, s1, jnp.where(r == 6, s1 * prev(a, 2), s2)))
    e2 = jnp.where(r <= 3, lower2, upper2)
    for j, e in enumerate((e0, e1, e2)):
        e_scr[j] = e.reshape(T, GLA_DK)
    ep_scr[...] = ep.reshape(T, GLA_DK)
    es_scr[...] = es.reshape(T, GLA_DK)

    for j in range(3, J):
        m = 1 << j
        for lo in range(0, T, 2 * m):
            mid, hi = lo + m, lo + 2 * m
            e_scr[j, lo:mid] = es_scr[lo:mid]
            e_scr[j, mid:hi] = ep_scr[mid:hi]
            lower_total = ep_scr[mid - 1:mid, :]
            upper_total = ep_scr[hi - 1:hi, :]
            ep_scr[mid:hi] = ep_scr[mid:hi] * lower_total
            es_scr[lo:mid] = es_scr[lo:mid] * upper_total


def _pool_mix(p, carry, mixed_scr, T):
    n = (T + POOL_CARRY) // SUBLANES
    s = jnp.concatenate([carry, p], axis=0).reshape(n, SUBLANES, POOL_WIDTH)
    tok = s[POOL_CARRY // SUBLANES:]
    r = jax.lax.broadcasted_iota(jnp.int32, (n, SUBLANES, LANES), 1)

    def shifted(x, k):
        down = lambda y: jnp.concatenate([y[:1], y[:-1]], axis=0)
        if k == SUBLANES:
            return down(x)
        xr = pltpu.roll(x, k, axis=1)
        rr = jnp.concatenate([r] * (x.shape[-1] // LANES), axis=-1)
        return jnp.where(rr >= k, xr, down(xr))

    width = 1
    for gi, w in enumerate(POOL_WINDOWS):
        assert w == 2 * width and w <= POOL_CARRY
        s = s + shifted(s, width)
        width = w
        mixed = s[POOL_CARRY // SUBLANES:, :, :POOL_GROUP] * (1.0 / w) - tok[:, :, gi * POOL_GROUP:(gi + 1) * POOL_GROUP]
        mixed_scr[:, gi * POOL_GROUP:(gi + 1) * POOL_GROUP] = mixed.reshape(T, POOL_GROUP)
        s = s[:, :, POOL_GROUP:]


def _tile_front(x, n1_ref, wgla_ref, wgate_ref, wpg_ref, wa2_ref, ba_ref, buf_ref,
                e_scr, ep_scr, es_scr, q_scr, k_scr, v_scr, r_scr, ga_scr, gb_scr, mixed_scr, merged_scr, T, J, S):
    xn = _rms(x, n1_ref[...]).astype(BF16)

    def proj(w_ref, lo, width):
        return _dot(xn, w_ref[:, lo:lo + width])

    a_lr = proj(wgate_ref, 0, GATE_PAD).astype(BF16)
    g = _log_sigmoid(_dot(a_lr, wa2_ref[...]) + ba_ref[...]) * (1.0 / GATE_TAU)

    q_scr[...] = proj(wgla_ref, C_Q, GLA_DK)
    k_scr[...] = proj(wgla_ref, C_K, GLA_DK)
    p = proj(wpg_ref, C_P, POOL_WIDTH)

    _decay_factors(g, e_scr, ep_scr, es_scr, T, J)
    Ts = T // S
    for i in range(S):
        _pool_mix(p[i * Ts:(i + 1) * Ts], buf_ref[i], mixed_scr.at[i * Ts:(i + 1) * Ts], Ts)
        buf_ref[i] = p[(i + 1) * Ts - POOL_CARRY:(i + 1) * Ts, :]
    return xn


def _tile_wide_projections(xn, wgla_ref, wpg_ref, v_scr, r_scr, ga_scr, gb_scr):
    v_scr[...] = _dot(xn, wgla_ref[:, C_V:C_V + GLA_DV])
    r_scr[...] = _dot(xn, wgla_ref[:, C_R:C_R + GLA_DV])
    ga_scr[...] = _dot(xn, wpg_ref[:, C_GA:C_GA + D_MODEL])
    gb_scr[...] = _dot(xn, wpg_ref[:, C_GB:C_GB + D_MODEL])


def _tile_attention(lvl_ref, e_scr, q_scr, k_scr, J):
    lvl = lvl_ref[...]
    atts = []
    for h in range(GLA_HEADS):
        ks = slice(h * DK_HEAD, (h + 1) * DK_HEAD)
        qh = q_scr[:, ks] * (DK_HEAD ** -0.5)
        kh = k_scr[:, ks]
        att = jnp.where(lvl == J, _dot_nt(qh.astype(BF16), kh.astype(BF16)), 0.0)
        for j in range(J):
            e = e_scr[j, :, ks]
            pj = _dot_nt((qh * e).astype(BF16), (kh * e).astype(BF16))
            att = jnp.where(lvl == j, pj, att)
        atts.append(att.astype(BF16))
    return atts


def _tile_back(x, atts, gng_ref, wpool_ref, pscale_ref, wout_ref, s_ref,
               e_scr, ep_scr, es_scr, q_scr, k_scr, v_scr, r_scr, ga_scr, gb_scr, mixed_scr, merged_scr, T, J, S):
    Ts = T // S
    streams = [slice(i * Ts, (i + 1) * Ts) for i in range(S)]
    for h in range(GLA_HEADS):
        ks = slice(h * DK_HEAD, (h + 1) * DK_HEAD)
        vs = slice(h * DV_HEAD, (h + 1) * DV_HEAD)
        qh = q_scr[:, ks] * (DK_HEAD ** -0.5)
        kh = k_scr[:, ks]
        vh = v_scr[:, vs].astype(BF16)

        q_dec = (qh * ep_scr[:, ks]).astype(BF16)
        o = jnp.concatenate([_dot(q_dec[rows], s_ref[i, h].astype(BF16)) for i, rows in enumerate(streams)], axis=0)
        o = o + _dot(atts[h], vh)

        k_dec = (kh * es_scr[:, ks]).astype(BF16)
        for i, rows in enumerate(streams):
            last = rows.stop - 1
            decay = jnp.broadcast_to(ep_scr[last:last + 1, ks], (DK_HEAD, DK_HEAD)).T
            decay = jnp.concatenate([decay] * (DV_HEAD // DK_HEAD), axis=1)
            s_ref[i, h] = decay * s_ref[i, h] + _dot_tn(k_dec[rows], vh[rows])

        o = o * jax.lax.rsqrt(jnp.mean(o * o, axis=-1, keepdims=True) + EPS) * gng_ref[:, vs]
        y_b = _dot(mixed_scr[:, ks].astype(BF16), wpool_ref[h])
        r = r_scr[:, vs]
        y_a = r * jax.nn.sigmoid(r) * o
        merged = jax.nn.sigmoid(ga_scr[:, vs]) * y_a + jax.nn.sigmoid(gb_scr[:, vs]) * (y_b * pscale_ref[:, vs])
        merged_scr[:, vs] = merged.astype(BF16)

    return x + _dot(merged_scr[...], wout_ref[...])


def _mixer_kernel(x_ref, s0_ref, buf0_ref, n1_ref, wgla_ref, wgate_ref, wpg_ref, wa2_ref, ba_ref, gng_ref,
                  wpool_ref, pscale_ref, wout_ref, lvl_ref,
                  h_ref, s_ref, buf_ref, *scratch, T, J, U, S):
    @pl.when(pl.program_id(1) == 0)
    def _():
        s_ref[...] = s0_ref[...]
        buf_ref[...] = buf0_ref[...]

    tiles = [[scr.at[u] for scr in scratch] for u in range(U)]
    Ts = T // S
    tokens = [slice(u * Ts, (u + 1) * Ts) for u in range(U)]
    xns = [_tile_front(x_ref[:, tokens[u], :].reshape(T, D_MODEL), n1_ref, wgla_ref, wgate_ref, wpg_ref, wa2_ref,
                       ba_ref, buf_ref, *tiles[u], T, J, S) for u in range(U)]
    for u in range(U):
        _tile_wide_projections(xns[u], wgla_ref, wpg_ref, *tiles[u][5:9])
    atts = [_tile_attention(lvl_ref, tiles[u][0], tiles[u][3], tiles[u][4], J) for u in range(U)]
    for u in range(U):
        out = _tile_back(x_ref[:, tokens[u], :].reshape(T, D_MODEL), atts[u], gng_ref, wpool_ref, pscale_ref,
                         wout_ref, s_ref, *tiles[u], T, J, S)
        h_ref[:, tokens[u], :] = out.reshape(S, Ts, D_MODEL)


def _const_spec(shape):
    zeros = (0,) * len(shape)
    return pl.BlockSpec(shape, lambda b, t: zeros, pipeline_mode=pl.Buffered(1))


def _mixer(x, s0, buf0, w, Ts, S, shared_state):
    B, L, D = x.shape
    assert L % Ts == 0 and Ts >= POOL_CARRY and B % S == 0 and (S == 1 or (L == Ts and not shared_state))
    T = S * Ts
    U = math.gcd(L // Ts, TILES_PER_STEP)
    J, lvl = _level_table(T, Ts)
    consts = (jnp.asarray(lvl),)

    state_idx = (lambda b, t: (0, 0, 0, 0)) if shared_state else (lambda b, t: (b, 0, 0, 0))
    buf_idx = (lambda b, t: (0, 0, 0)) if shared_state else (lambda b, t: (b, 0, 0))
    weights = (w["norm1_g"], w["w_in"], w["w_gate"], w["w_poolgate"], w["w_alpha2"], w["b_alpha"], w["gla_norm_g"],
               w["w_pool"], w["pool_scale"], w["w_out"]) + consts
    weight_blocks = [a.shape for a in weights]
    weight_blocks[1] = (D_MODEL, W_GLA_COLS)
    in_specs = [
        pl.BlockSpec((S, U * Ts, D), lambda b, t: (b, t, 0)),
        pl.BlockSpec((S, GLA_HEADS, DK_HEAD, DV_HEAD), state_idx),
        pl.BlockSpec((S, POOL_CARRY, POOL_WIDTH), buf_idx),
    ] + [_const_spec(shape) for shape in weight_blocks]
    out_shape = (
        jax.ShapeDtypeStruct((B, L, D), F32),
        jax.ShapeDtypeStruct((B, GLA_HEADS, DK_HEAD, DV_HEAD), F32),
        jax.ShapeDtypeStruct((B, POOL_CARRY, POOL_WIDTH), F32),
    )
    out_specs = (
        pl.BlockSpec((S, U * Ts, D), lambda b, t: (b, t, 0)),
        pl.BlockSpec((S, GLA_HEADS, DK_HEAD, DV_HEAD), lambda b, t: (b, 0, 0, 0)),
        pl.BlockSpec((S, POOL_CARRY, POOL_WIDTH), lambda b, t: (b, 0, 0)),
    )
    scratch = [
        pltpu.VMEM((U, J, T, GLA_DK), F32),
        pltpu.VMEM((U, T, GLA_DK), F32),
        pltpu.VMEM((U, T, GLA_DK), F32),
        pltpu.VMEM((U, T, GLA_DK), F32),
        pltpu.VMEM((U, T, GLA_DK), F32),
        pltpu.VMEM((U, T, GLA_DV), F32),
        pltpu.VMEM((U, T, GLA_DV), F32),
        pltpu.VMEM((U, T, D_MODEL), F32),
        pltpu.VMEM((U, T, D_MODEL), F32),
        pltpu.VMEM((U, T, POOL_WIDTH), F32),
        pltpu.VMEM((U, T, D_MODEL), BF16),
    ]
    return pl.pallas_call(
        functools.partial(_mixer_kernel, T=T, J=J, U=U, S=S),
        out_shape=out_shape,
        grid=(B // S, L // (U * Ts)),
        in_specs=in_specs,
        out_specs=out_specs,
        scratch_shapes=scratch,
        compiler_params=pltpu.CompilerParams(
            dimension_semantics=("arbitrary", "arbitrary"),
            vmem_limit_bytes=VMEM_LIMIT_BYTES),
        name=f"mixer_s{S}t{Ts}x{U}",
    )(x, s0, buf0, *weights)


def _ffn_kernel(h_ref, n2_ref, wg_ref, wu_ref, wd_ref, nf_ref, y_ref):
    R = h_ref.shape[0] // FFN_PIECES
    rows = [slice(i * R, (i + 1) * R) for i in range(FFN_PIECES)]
    hs = [h_ref[r, :] for r in rows]
    hns = [_rms(h, n2_ref[...]).astype(BF16) for h in hs]
    for i in range(FFN_PIECES):
        gate = _dot(hns[i], wg_ref[...])
        up = _dot(hns[i], wu_ref[...])
        act = (gate * jax.nn.sigmoid(gate) * up).astype(BF16)
        h2 = hs[i] + _dot(act, wd_ref[...])
        y_ref[rows[i], :] = _rms(h2, nf_ref[...])


def _ffn(h, w, R):
    N, D = h.shape
    assert N % R == 0
    weights = (w["norm2_g"], w["w_ffn_gate"], w["w_ffn_up"], w["w_ffn_down"], w["norm_f_g"])

    def const(a):
        zeros = (0,) * a.ndim
        return pl.BlockSpec(a.shape, lambda i: zeros, pipeline_mode=pl.Buffered(1))

    return pl.pallas_call(
        _ffn_kernel,
        out_shape=jax.ShapeDtypeStruct((N, D), F32),
        grid=(N // R,),
        in_specs=[pl.BlockSpec((R, D), lambda i: (i, 0))] + [const(a) for a in weights],
        out_specs=pl.BlockSpec((R, D), lambda i: (i, 0)),
        compiler_params=pltpu.CompilerParams(
            dimension_semantics=("arbitrary",),
            vmem_limit_bytes=VMEM_LIMIT_BYTES),
        name=f"ffn_r{R}",
    )(h, *weights)


def _pack_weights(norm1_g, w_in, w_alpha2, b_alpha, gla_norm_g, w_pool, pool_scale, w_out,
                  norm2_g, w_ffn_gate, w_ffn_up, w_ffn_down, norm_f_g):
    o_p = W_GLA_COLS + GATE_RANK
    w_in_bf = w_in.astype(BF16)
    w_gate = jnp.pad(w_in_bf[:, W_GLA_COLS:o_p], ((0, 0), (0, GATE_PAD - GATE_RANK)))
    w_poolgate = w_in_bf[:, o_p:]
    w_alpha2_packed = jnp.pad(w_alpha2, ((0, GATE_PAD - GATE_RANK), (0, 0))).astype(BF16)
    row = lambda v: v.reshape(1, -1).astype(F32)
    return dict(
        norm1_g=row(norm1_g), w_in=w_in_bf, w_gate=w_gate, w_poolgate=w_poolgate,
        w_alpha2=w_alpha2_packed, b_alpha=row(b_alpha),
        gla_norm_g=row(gla_norm_g), w_pool=w_pool.astype(BF16), pool_scale=row(pool_scale),
        w_out=w_out.astype(BF16), norm2_g=row(norm2_g), w_ffn_gate=w_ffn_gate.astype(BF16),
        w_ffn_up=w_ffn_up.astype(BF16), w_ffn_down=w_ffn_down.astype(BF16), norm_f_g=row(norm_f_g))


def _largest_tile(n, cap):
    t = cap
    while n % t:
        t //= 2
    return t


def kernel(x_prompt, x_sample, state_gla, state_pool, meta_tokens, norm1_g, w_in, w_alpha2, b_alpha,
           gla_norm_g, w_pool, pool_scale, w_out, norm2_g, w_ffn_gate, w_ffn_up, w_ffn_down, norm_f_g):
    w = _pack_weights(norm1_g, w_in, w_alpha2, b_alpha, gla_norm_g, w_pool, pool_scale, w_out,
                      norm2_g, w_ffn_gate, w_ffn_up, w_ffn_down, norm_f_g)
    B, L, D = x_prompt.shape
    Bs, Ls, _ = x_sample.shape
    n_meta = meta_tokens.shape[0]
    assert POOL_BUF <= n_meta <= META_TILE and PAST_LEN >= POOL_BUF

    s_zero = jnp.zeros((1, GLA_HEADS, DK_HEAD, DV_HEAD), F32)
    buf_zero = jnp.zeros((1, POOL_CARRY, POOL_WIDTH), F32)
    x_meta = jnp.pad(meta_tokens.astype(F32), ((META_TILE - n_meta, 0), (0, 0)))[None]
    _, s_meta, buf_meta = _mixer(x_meta, s_zero, buf_zero, w, META_TILE, 1, True)

    h_p, s_p, buf_p = _mixer(x_prompt, s_meta, buf_meta, w, _largest_tile(L, PROMPT_TILE), 1, True)

    buf_s0 = jnp.pad(state_pool.astype(F32), ((0, 0), (POOL_CARRY - POOL_BUF, 0), (0, 0)))
    ts = _largest_tile(Ls, PROMPT_TILE)
    streams = math.gcd(Bs, PROMPT_TILE // ts) if ts == Ls else 1
    h_s, s_s, buf_s = _mixer(x_sample, state_gla.astype(F32), buf_s0, w, ts, streams, False)

    y_p = _ffn(h_p.reshape(B * L, D), w, _largest_tile(B * L, FFN_TILE)).reshape(B, L, D)
    y_s = _ffn(h_s.reshape(Bs * Ls, D), w, _largest_tile(Bs * Ls, FFN_TILE)).reshape(Bs, Ls, D)
    return (y_p, y_s,
            s_p.astype(state_gla.dtype), buf_p[:, POOL_CARRY - POOL_BUF:].astype(state_pool.dtype),
            s_s.astype(state_gla.dtype), buf_s[:, POOL_CARRY - POOL_BUF:].astype(state_pool.dtype))
```

```python
import functools
import math

import numpy as np
import jax
import jax.numpy as jnp
from jax.experimental import pallas as pl
from jax.experimental.pallas import tpu as pltpu

D_MODEL = 1024
GLA_HEADS = 4
GLA_DK = D_MODEL // 2
GLA_DV = D_MODEL
DK_HEAD = GLA_DK // GLA_HEADS
DV_HEAD = GLA_DV // GLA_HEADS
GATE_RANK = 16
GATE_TAU = 16.0
POOL_WIDTH = D_MODEL // 2
POOL_WINDOWS = (2, 4, 8, 16)
POOL_GROUPS = len(POOL_WINDOWS)
POOL_GROUP = POOL_WIDTH // POOL_GROUPS
POOL_BUF = max(POOL_WINDOWS) - 1
POOL_CARRY = POOL_BUF + 1
PAST_LEN = 1024
EPS = 1e-6

LANES = 128
SUBLANES = 8
GATE_PAD = LANES

C_Q = 0
C_K = C_Q + GLA_DK
C_V = C_K + GLA_DK
C_R = C_V + GLA_DV
W_GLA_COLS = C_R + GLA_DV
C_P = 0
C_GA = C_P + POOL_WIDTH
C_GB = C_GA + D_MODEL

VMEM_LIMIT_BYTES = 56 * 1024 * 1024

PROMPT_TILE = 256
TILES_PER_STEP = 2
META_TILE = 128
FFN_TILE = 512
FFN_PIECES = 2

F32 = jnp.float32
BF16 = jnp.bfloat16


def _level_table(T, block):
    J = int(math.log2(block))
    assert 1 << J == block and block >= 2 * SUBLANES and T % block == 0
    t = np.arange(T)[:, None]
    u = np.arange(T)[None, :]
    hb = np.floor(np.log2(np.maximum(t ^ u, 1))).astype(np.int32)
    lvl = np.where((u > t) | (t // block != u // block), -1, np.where(u == t, J, hb)).astype(np.int32)
    return J, lvl


def _rms(x, g):
    return x * jax.lax.rsqrt(jnp.mean(x * x, axis=-1, keepdims=True) + EPS) * g


def _dot(a, b):
    return jnp.dot(a, b, preferred_element_type=F32)


def _dot_nt(a, b):
    return jax.lax.dot_general(a, b, (((1,), (1,)), ((), ())), preferred_element_type=F32)


def _dot_tn(a, b):
    return jax.lax.dot_general(a, b, (((0,), (0,)), ((), ())), preferred_element_type=F32)


def _log_sigmoid(z):
    return jnp.minimum(z, 0.0) - jnp.log(1.0 + jnp.exp(-jnp.abs(z)))


def _decay_factors(g, e_scr, ep_scr, es_scr, T, J):
    a = jnp.exp(g).reshape(T // SUBLANES, SUBLANES, GLA_DK)
    r = jax.lax.broadcasted_iota(jnp.int32, a.shape, 1)
    one = jnp.ones_like(a)

    def prev(x, s):
        return pltpu.roll(x, s, axis=1)

    def nxt(x, s):
        return pltpu.roll(x, SUBLANES - s, axis=1)

    s1 = a * jnp.where(r >= 1, prev(a, 1), one)
    s2 = s1 * jnp.where(r >= 2, prev(s1, 2), one)
    ep = s2 * jnp.where(r >= 4, prev(s2, 4), one)
    c = jnp.where(r <= 6, nxt(a, 1), one)
    d1 = c * jnp.where(r <= 6, nxt(c, 1), one)
    d2 = d1 * jnp.where(r <= 5, nxt(d1, 2), one)
    es = d2 * jnp.where(r <= 3, nxt(d2, 4), one)

    r4 = r & 3
    e0 = jnp.where((r & 1) == 1, a, one)
    e1 = jnp.where(r4 == 0, c, jnp.where(r4 == 1, one, jnp.where(r4 == 2, a, s1)))
    lower2 = jnp.where(r == 0, d1 * nxt(c, 2), jnp.where(r == 1, d1, jnp.where(r == 2, c, one)))
    upper2 = jnp.where(r == 4, a, jnp.where(r == 5, s1, jnp.where(r == 6, s1 * prev(a, 2), s2)))
    e2 = jnp.where(r <= 3, lower2, upper2)
    for j, e in enumerate((e0, e1, e2)):
        e_scr[j] = e.reshape(T, GLA_DK)
    ep_scr[...] = ep.reshape(T, GLA_DK)
    es_scr[...] = es.reshape(T, GLA_DK)

    for j in range(3, J):
        m = 1 << j
        for lo in range(0, T, 2 * m):
            mid, hi = lo + m, lo + 2 * m
            e_scr[j, lo:mid] = es_scr[lo:mid]
            e_scr[j, mid:hi] = ep_scr[mid:hi]
            lower_total = ep_scr[mid - 1:mid, :]
            upper_total = ep_scr[hi - 1:hi, :]
            ep_scr[mid:hi] = ep_scr[mid:hi] * lower_total
            es_scr[lo:mid] = es_scr[lo:mid] * upper_total


def _pool_mix(p, carry, mixed_scr, T):
    n = (T + POOL_CARRY) // SUBLANES
    s = jnp.concatenate([carry, p], axis=0).reshape(n, SUBLANES, POOL_WIDTH)
    tok = s[POOL_CARRY // SUBLANES:]
    r = jax.lax.broadcasted_iota(jnp.int32, (n, SUBLANES, LANES), 1)

    def shifted(x, k):
        down = lambda y: jnp.concatenate([y[:1], y[:-1]], axis=0)
        if k == SUBLANES:
            return down(x)
        xr = pltpu.roll(x, k, axis=1)
        rr = jnp.concatenate([r] * (x.shape[-1] // LANES), axis=-1)
        return jnp.where(rr >= k, xr, down(xr))

    width = 1
    for gi, w in enumerate(POOL_WINDOWS):
        assert w == 2 * width and w <= POOL_CARRY
        s = s + shifted(s, width)
        width = w
        mixed = s[POOL_CARRY // SUBLANES:, :, :POOL_GROUP] * (1.0 / w) - tok[:, :, gi * POOL_GROUP:(gi + 1) * POOL_GROUP]
        mixed_scr[:, gi * POOL_GROUP:(gi + 1) * POOL_GROUP] = mixed.reshape(T, POOL_GROUP)
        s = s[:, :, POOL_GROUP:]


def _tile_front(x, n1_ref, wgla_ref, wgate_ref, wpg_ref, wa2_ref, ba_ref, buf_ref,
                e_scr, ep_scr, es_scr, q_scr, k_scr, v_scr, r_scr, ga_scr, gb_scr, mixed_scr, merged_scr, T, J, S):
    xn = _rms(x, n1_ref[...]).astype(BF16)

    def proj(w_ref, lo, width):
        return _dot(xn, w_ref[:, lo:lo + width])

    a_lr = proj(wgate_ref, 0, GATE_PAD).astype(BF16)
    g = _log_sigmoid(_dot(a_lr, wa2_ref[...]) + ba_ref[...]) * (1.0 / GATE_TAU)

    q_scr[...] = proj(wgla_ref, C_Q, GLA_DK)
    k_scr[...] = proj(wgla_ref, C_K, GLA_DK)
    p = proj(wpg_ref, C_P, POOL_WIDTH)

    _decay_factors(g, e_scr, ep_scr, es_scr, T, J)
    Ts = T // S
    for i in range(S):
        _pool_mix(p[i * Ts:(i + 1) * Ts], buf_ref[i], mixed_scr.at[i * Ts:(i + 1) * Ts], Ts)
        buf_ref[i] = p[(i + 1) * Ts - POOL_CARRY:(i + 1) * Ts, :]
    return xn


def _tile_wide_projections(xn, wgla_ref, wpg_ref, v_scr, r_scr, ga_scr, gb_scr):
    v_scr[...] = _dot(xn, wgla_ref[:, C_V:C_V + GLA_DV])
    r_scr[...] = _dot(xn, wgla_ref[:, C_R:C_R + GLA_DV])
    ga_scr[...] = _dot(xn, wpg_ref[:, C_GA:C_GA + D_MODEL])
    gb_scr[...] = _dot(xn, wpg_ref[:, C_GB:C_GB + D_MODEL])


def _tile_gates(wpool_ref, pscale_ref, r_scr, ga_scr, gb_scr, mixed_scr):
    for h in range(GLA_HEADS):
        ks = slice(h * DK_HEAD, (h + 1) * DK_HEAD)
        vs = slice(h * DV_HEAD, (h + 1) * DV_HEAD)
        r = r_scr[:, vs]
        r_scr[:, vs] = r * jax.nn.sigmoid(r)
        ga_scr[:, vs] = jax.nn.sigmoid(ga_scr[:, vs])
        y_b = _dot(mixed_scr[:, ks].astype(BF16), wpool_ref[h])
        gb_scr[:, vs] = jax.nn.sigmoid(gb_scr[:, vs]) * (y_b * pscale_ref[:, vs])


def _tile_attention(lvl_ref, e_scr, q_scr, k_scr, J):
    lvl = lvl_ref[...]
    atts = []
    for h in range(GLA_HEADS):
        ks = slice(h * DK_HEAD, (h + 1) * DK_HEAD)
        qh = q_scr[:, ks] * (DK_HEAD ** -0.5)
        kh = k_scr[:, ks]
        att = jnp.where(lvl == J, _dot_nt(qh.astype(BF16), kh.astype(BF16)), 0.0)
        for j in range(J):
            e = e_scr[j, :, ks]
            pj = _dot_nt((qh * e).astype(BF16), (kh * e).astype(BF16))
            att = jnp.where(lvl == j, pj, att)
        atts.append(att.astype(BF16))
    return atts


def _tile_back(x, atts, gng_ref, wpool_ref, pscale_ref, wout_ref, s_ref,
               e_scr, ep_scr, es_scr, q_scr, k_scr, v_scr, r_scr, ga_scr, gb_scr, mixed_scr, merged_scr, T, J, S):
    Ts = T // S
    streams = [slice(i * Ts, (i + 1) * Ts) for i in range(S)]
    for h in range(GLA_HEADS):
        ks = slice(h * DK_HEAD, (h + 1) * DK_HEAD)
        vs = slice(h * DV_HEAD, (h + 1) * DV_HEAD)
        qh = q_scr[:, ks] * (DK_HEAD ** -0.5)
        kh = k_scr[:, ks]
        vh = v_scr[:, vs].astype(BF16)

        q_dec = (qh * ep_scr[:, ks]).astype(BF16)
        o = jnp.concatenate([_dot(q_dec[rows], s_ref[i, h].astype(BF16)) for i, rows in enumerate(streams)], axis=0)
        o = o + _dot(atts[h], vh)

        k_dec = (kh * es_scr[:, ks]).astype(BF16)
        for i, rows in enumerate(streams):
            last = rows.stop - 1
            decay = jnp.broadcast_to(ep_scr[last:last + 1, ks], (DK_HEAD, DK_HEAD)).T
            decay = jnp.concatenate([decay] * (DV_HEAD // DK_HEAD), axis=1)
            s_ref[i, h] = decay * s_ref[i, h] + _dot_tn(k_dec[rows], vh[rows])

        o = o * jax.lax.rsqrt(jnp.mean(o * o, axis=-1, keepdims=True) + EPS) * gng_ref[:, vs]
        merged = ga_scr[:, vs] * (r_scr[:, vs] * o) + gb_scr[:, vs]
        merged_scr[:, vs] = merged.astype(BF16)

    return x + _dot(merged_scr[...], wout_ref[...])


def _mixer_kernel(x_ref, s0_ref, buf0_ref, n1_ref, wgla_ref, wgate_ref, wpg_ref, wa2_ref, ba_ref, gng_ref,
                  wpool_ref, pscale_ref, wout_ref, lvl_ref,
                  h_ref, s_ref, buf_ref, *scratch, T, J, U, S):
    @pl.when(pl.program_id(1) == 0)
    def _():
        s_ref[...] = s0_ref[...]
        buf_ref[...] = buf0_ref[...]

    tiles = [[scr.at[u] for scr in scratch] for u in range(U)]
    Ts = T // S
    tokens = [slice(u * Ts, (u + 1) * Ts) for u in range(U)]
    xns = [_tile_front(x_ref[:, tokens[u], :].reshape(T, D_MODEL), n1_ref, wgla_ref, wgate_ref, wpg_ref, wa2_ref,
                       ba_ref, buf_ref, *tiles[u], T, J, S) for u in range(U)]
    for u in range(U):
        _tile_wide_projections(xns[u], wgla_ref, wpg_ref, *tiles[u][5:9])
    for u in range(U):
        _tile_gates(wpool_ref, pscale_ref, *tiles[u][6:10])
    atts =[_tile_attention(lvl_ref, tiles[u][0], tiles[u][3], tiles[u][4], J) for u in range(U)]
    for u in range(U):
        out = _tile_back(x_ref[:, tokens[u], :].reshape(T, D_MODEL), atts[u], gng_ref, wpool_ref, pscale_ref,
                         wout_ref, s_ref, *tiles[u], T, J, S)
        h_ref[:, tokens[u], :] = out.reshape(S, Ts, D_MODEL)


def _const_spec(shape):
    zeros = (0,) * len(shape)
    return pl.BlockSpec(shape, lambda b, t: zeros, pipeline_mode=pl.Buffered(1))


def _mixer(x, s0, buf0, w, Ts, S, shared_state):
    B, L, D = x.shape
    assert L % Ts == 0 and Ts >= POOL_CARRY and B % S == 0 and (S == 1 or (L == Ts and not shared_state))
    T = S * Ts
    U = math.gcd(L // Ts, TILES_PER_STEP)
    J, lvl = _level_table(T, Ts)
    consts = (jnp.asarray(lvl),)

    state_idx = (lambda b, t: (0, 0, 0, 0)) if shared_state else (lambda b, t: (b, 0, 0, 0))
    buf_idx = (lambda b, t: (0, 0, 0)) if shared_state else (lambda b, t: (b, 0, 0))
    weights = (w["norm1_g"], w["w_in"], w["w_gate"], w["w_poolgate"], w["w_alpha2"], w["b_alpha"], w["gla_norm_g"],
               w["w_pool"], w["pool_scale"], w["w_out"]) + consts
    weight_blocks = [a.shape for a in weights]
    weight_blocks[1] = (D_MODEL, W_GLA_COLS)
    in_specs = [
        pl.BlockSpec((S, U * Ts, D), lambda b, t: (b, t, 0)),
        pl.BlockSpec((S, GLA_HEADS, DK_HEAD, DV_HEAD), state_idx),
        pl.BlockSpec((S, POOL_CARRY, POOL_WIDTH), buf_idx),
    ] + [_const_spec(shape) for shape in weight_blocks]
    out_shape = (
        jax.ShapeDtypeStruct((B, L, D), F32),
        jax.ShapeDtypeStruct((B, GLA_HEADS, DK_HEAD, DV_HEAD), F32),
        jax.ShapeDtypeStruct((B, POOL_CARRY, POOL_WIDTH), F32),
    )
    out_specs = (
        pl.BlockSpec((S, U * Ts, D), lambda b, t: (b, t, 0)),
        pl.BlockSpec((S, GLA_HEADS, DK_HEAD, DV_HEAD), lambda b, t: (b, 0, 0, 0)),
        pl.BlockSpec((S, POOL_CARRY, POOL_WIDTH), lambda b, t: (b, 0, 0)),
    )
    scratch = [
        pltpu.VMEM((U, J, T, GLA_DK), F32),
        pltpu.VMEM((U, T, GLA_DK), F32),
        pltpu.VMEM((U, T, GLA_DK), F32),
        pltpu.VMEM((U, T, GLA_DK), F32),
        pltpu.VMEM((U, T, GLA_DK), F32),
        pltpu.VMEM((U, T, GLA_DV), F32),
        pltpu.VMEM((U, T, GLA_DV), F32),
        pltpu.VMEM((U, T, D_MODEL), F32),
        pltpu.VMEM((U, T, D_MODEL), F32),
        pltpu.VMEM((U, T, POOL_WIDTH), F32),
        pltpu.VMEM((U, T, D_MODEL), BF16),
    ]
    return pl.pallas_call(
        functools.partial(_mixer_kernel, T=T, J=J, U=U, S=S),
        out_shape=out_shape,
        grid=(B // S, L // (U * Ts)),
        in_specs=in_specs,
        out_specs=out_specs,
        scratch_shapes=scratch,
        compiler_params=pltpu.CompilerParams(
            dimension_semantics=("arbitrary", "arbitrary"),
            vmem_limit_bytes=VMEM_LIMIT_BYTES),
        name=f"mixer_s{S}t{Ts}x{U}",
    )(x, s0, buf0, *weights)


def _ffn_kernel(h_ref, n2_ref, wg_ref, wu_ref, wd_ref, nf_ref, y_ref):
    R = h_ref.shape[0] // FFN_PIECES
    rows = [slice(i * R, (i + 1) * R) for i in range(FFN_PIECES)]
    hs = [h_ref[r, :] for r in rows]
    hns = [_rms(h, n2_ref[...]).astype(BF16) for h in hs]
    for i in range(FFN_PIECES):
        gate = _dot(hns[i], wg_ref[...])
        up = _dot(hns[i], wu_ref[...])
        act = (gate * jax.nn.sigmoid(gate) * up).astype(BF16)
        h2 = hs[i] + _dot(act, wd_ref[...])
        y_ref[rows[i], :] = _rms(h2, nf_ref[...])


def _ffn(h, w, R):
    N, D = h.shape
    assert N % R == 0
    weights = (w["norm2_g"], w["w_ffn_gate"], w["w_ffn_up"], w["w_ffn_down"], w["norm_f_g"])

    def const(a):
        zeros = (0,) * a.ndim
        return pl.BlockSpec(a.shape, lambda i: zeros, pipeline_mode=pl.Buffered(1))

    return pl.pallas_call(
        _ffn_kernel,
        out_shape=jax.ShapeDtypeStruct((N, D), F32),
        grid=(N // R,),
        in_specs=[pl.BlockSpec((R, D), lambda i: (i, 0))] + [const(a) for a in weights],
        out_specs=pl.BlockSpec((R, D), lambda i: (i, 0)),
        compiler_params=pltpu.CompilerParams(
            dimension_semantics=("arbitrary",),
            vmem_limit_bytes=VMEM_LIMIT_BYTES),
        name=f"ffn_r{R}",
    )(h, *weights)


def _pack_weights(norm1_g, w_in, w_alpha2, b_alpha, gla_norm_g, w_pool, pool_scale, w_out,
                  norm2_g, w_ffn_gate, w_ffn_up, w_ffn_down, norm_f_g):
    o_p = W_GLA_COLS + GATE_RANK
    w_in_bf = w_in.astype(BF16)
    w_gate = jnp.pad(w_in_bf[:, W_GLA_COLS:o_p], ((0, 0), (0, GATE_PAD - GATE_RANK)))
    w_poolgate = w_in_bf[:, o_p:]
    w_alpha2_packed = jnp.pad(w_alpha2, ((0, GATE_PAD - GATE_RANK), (0, 0))).astype(BF16)
    row = lambda v: v.reshape(1, -1).astype(F32)
    return dict(
        norm1_g=row(norm1_g), w_in=w_in_bf, w_gate=w_gate, w_poolgate=w_poolgate,
        w_alpha2=w_alpha2_packed, b_alpha=row(b_alpha),
        gla_norm_g=row(gla_norm_g), w_pool=w_pool.astype(BF16), pool_scale=row(pool_scale),
        w_out=w_out.astype(BF16), norm2_g=row(norm2_g), w_ffn_gate=w_ffn_gate.astype(BF16),
        w_ffn_up=w_ffn_up.astype(BF16), w_ffn_down=w_ffn_down.astype(BF16), norm_f_g=row(norm_f_g))


def _largest_tile(n, cap):
    t = cap
    while n % t:
        t //= 2
    return t


def kernel(x_prompt, x_sample, state_gla, state_pool, meta_tokens, norm1_g, w_in, w_alpha2, b_alpha,
           gla_norm_g, w_pool, pool_scale, w_out, norm2_g, w_ffn_gate, w_ffn_up, w_ffn_down, norm_f_g):
    w = _pack_weights(norm1_g, w_in, w_alpha2, b_alpha, gla_norm_g, w_pool, pool_scale, w_out,
                      norm2_g, w_ffn_gate, w_ffn_up, w_ffn_down, norm_f_g)
    B, L, D = x_prompt.shape
    Bs, Ls, _ = x_sample.shape
    n_meta = meta_tokens.shape[0]
    assert POOL_BUF <= n_meta <= META_TILE and PAST_LEN >= POOL_BUF

    s_zero = jnp.zeros((1, GLA_HEADS, DK_HEAD, DV_HEAD), F32)
    buf_zero = jnp.zeros((1, POOL_CARRY, POOL_WIDTH), F32)
    x_meta = jnp.pad(meta_tokens.astype(F32), ((META_TILE - n_meta, 0), (0, 0)))[None]
    _, s_meta, buf_meta = _mixer(x_meta, s_zero, buf_zero, w, META_TILE, 1, True)

    h_p, s_p, buf_p = _mixer(x_prompt, s_meta, buf_meta, w, _largest_tile(L, PROMPT_TILE), 1, True)

    buf_s0 = jnp.pad(state_pool.astype(F32), ((0, 0), (POOL_CARRY - POOL_BUF, 0), (0, 0)))
    ts = _largest_tile(Ls, PROMPT_TILE)
    streams = math.gcd(Bs, PROMPT_TILE // ts) if ts == Ls else 1
    h_s, s_s, buf_s = _mixer(x_sample, state_gla.astype(F32), buf_s0, w, ts, streams, False)

    y_p = _ffn(h_p.reshape(B * L, D), w, _largest_tile(B * L, FFN_TILE)).reshape(B, L, D)
    y_s = _ffn(h_s.reshape(Bs * Ls, D), w, _largest_tile(Bs * Ls, FFN_TILE)).reshape(Bs, Ls, D)
    return (y_p, y_s,
            s_p.astype(state_gla.dtype), buf_p[:, POOL_CARRY - POOL_BUF:].astype(state_pool.dtype),
            s_s.astype(state_gla.dtype), buf_s[:, POOL_CARRY - POOL_BUF:].astype(state_pool.dtype))
```

```python
import functools
import math

import numpy as np
import jax
import jax.numpy as jnp
from jax.experimental import pallas as pl
from jax.experimental.pallas import tpu as pltpu

D_MODEL = 1024
GLA_HEADS = 4
GLA_DK = D_MODEL // 2
GLA_DV = D_MODEL
DK_HEAD = GLA_DK // GLA_HEADS
DV_HEAD = GLA_DV // GLA_HEADS
GATE_RANK = 16
GATE_TAU = 16.0
POOL_WIDTH = D_MODEL // 2
POOL_WINDOWS = (2, 4, 8, 16)
POOL_GROUPS = len(POOL_WINDOWS)
POOL_GROUP = POOL_WIDTH // POOL_GROUPS
POOL_BUF = max(POOL_WINDOWS) - 1
POOL_CARRY = POOL_BUF + 1
PAST_LEN = 1024
EPS = 1e-6

LANES = 128
SUBLANES = 8
GATE_PAD = LANES

C_Q = 0
C_K = C_Q + GLA_DK
C_V = C_K + GLA_DK
C_R = C_V + GLA_DV
W_GLA_COLS = C_R + GLA_DV
C_P = 0
C_GA = C_P + POOL_WIDTH
C_GB = C_GA + D_MODEL

VMEM_LIMIT_BYTES = 56 * 1024 * 1024

PROMPT_TILE = 256
TILES_PER_STEP = 2
META_TILE = 128
FFN_TILE = 512
FFN_PIECES = 2

F32 = jnp.float32
BF16 = jnp.bfloat16


def _level_table(T, block):
    J = int(math.log2(block))
    assert 1 << J == block and block >= 2 * SUBLANES and T % block == 0
    t = np.arange(T)[:, None]
    u = np.arange(T)[None, :]
    hb = np.floor(np.log2(np.maximum(t ^ u, 1))).astype(np.int32)
    lvl = np.where((u > t) | (t // block != u // block), -1, np.where(u == t, J, hb)).astype(np.int32)
    return J, lvl


def _rms(x, g):
    return x * jax.lax.rsqrt(jnp.mean(x * x, axis=-1, keepdims=True) + EPS) * g


def _dot(a, b):
    return jnp.dot(a, b, preferred_element_type=F32)


def _dot_nt(a, b):
    return jax.lax.dot_general(a, b, (((1,), (1,)), ((), ())), preferred_element_type=F32)


def _dot_tn(a, b):
    return jax.lax.dot_general(a, b, (((0,), (0,)), ((), ())), preferred_element_type=F32)


def _log_sigmoid(z):
    return jnp.minimum(z, 0.0) - jnp.log(1.0 + jnp.exp(-jnp.abs(z)))


def _decay_factors(g, e_scr, ep_scr, es_scr, T, J):
    assert SUBLANES == 8
    a = jnp.exp(g).reshape(T // SUBLANES, SUBLANES, GLA_DK)
    r = jax.lax.broadcasted_iota(jnp.int32, a.shape, 1)
    one = jnp.ones_like(a)

    def prev(x, s):
        return pltpu.roll(x, s, axis=1)

    def nxt(x, s):
        return pltpu.roll(x, SUBLANES - s, axis=1)

    s1 = a * jnp.where(r >= 1, prev(a, 1), one)
    s2 = s1 * jnp.where(r >= 2, prev(s1, 2), one)
    ep = s2 * jnp.where(r >= 4, prev(s2, 4), one)
    c = jnp.where(r <= 6, nxt(a, 1), one)
    d1 = c * jnp.where(r <= 6, nxt(c, 1), one)
    d2 = d1 * jnp.where(r <= 5, nxt(d1, 2), one)
    es = d2 * jnp.where(r <= 3, nxt(d2, 4), one)

    r4 = r & 3
    e0 = jnp.where((r & 1) == 1, a, one)
    e1 = jnp.where(r4 == 0, c, jnp.where(r4 == 1, one, jnp.where(r4 == 2, a, s1)))
    lower2 = jnp.where(r == 0, d1 * nxt(c, 2), jnp.where(r == 1, d1, jnp.where(r == 2, c, one)))
    upper2 = jnp.where(r == 4, a, jnp.where(r == 5, s1, jnp.where(r == 6, s1 * prev(a, 2), s2)))
    e2 = jnp.where(r <= 3, lower2, upper2)
    for j, e in enumerate((e0, e1, e2)):
        e_scr[j] = e.reshape(T, GLA_DK)
    ep_scr[...] = ep.reshape(T, GLA_DK)
    es_scr[...] = es.reshape(T, GLA_DK)

    for j in range(3, J):
        m = 1 << j
        for lo in range(0, T, 2 * m):
            mid, hi = lo + m, lo + 2 * m
            e_scr[j, lo:mid] = es_scr[lo:mid]
            e_scr[j, mid:hi] = ep_scr[mid:hi]
            lower_total = ep_scr[mid - 1:mid, :]
            upper_total = ep_scr[hi - 1:hi, :]
            ep_scr[mid:hi] = ep_scr[mid:hi] * lower_total
            es_scr[lo:mid] = es_scr[lo:mid] * upper_total


def _pool_mix(p, carry, mixed_scr, T):
    n = (T + POOL_CARRY) // SUBLANES
    s = jnp.concatenate([carry, p], axis=0).reshape(n, SUBLANES, POOL_WIDTH)
    tok = s[POOL_CARRY // SUBLANES:]
    r = jax.lax.broadcasted_iota(jnp.int32, (n, SUBLANES, LANES), 1)

    def shifted(x, k):
        down = lambda y: jnp.concatenate([y[:1], y[:-1]], axis=0)
        if k == SUBLANES:
            return down(x)
        xr = pltpu.roll(x, k, axis=1)
        rr = jnp.concatenate([r] * (x.shape[-1] // LANES), axis=-1)
        return jnp.where(rr >= k, xr, down(xr))

    width = 1
    for gi, w in enumerate(POOL_WINDOWS):
        assert w == 2 * width and w <= POOL_CARRY
        s = s + shifted(s, width)
        width = w
        mixed = s[POOL_CARRY // SUBLANES:, :, :POOL_GROUP] * (1.0 / w) - tok[:, :, gi * POOL_GROUP:(gi + 1) * POOL_GROUP]
        mixed_scr[:, gi * POOL_GROUP:(gi + 1) * POOL_GROUP] = mixed.reshape(T, POOL_GROUP)
        s = s[:, :, POOL_GROUP:]


def _tile_front(x, n1_ref, wgla_ref, wgate_ref, wpg_ref, wa2_ref, ba_ref, buf_ref,
                e_scr, ep_scr, es_scr, q_scr, k_scr, v_scr, r_scr, ga_scr, gb_scr, mixed_scr, merged_scr, T, J, S):
    xn = _rms(x, n1_ref[...]).astype(BF16)

    def proj(w_ref, lo, width):
        return _dot(xn, w_ref[:, lo:lo + width])

    a_lr = proj(wgate_ref, 0, GATE_PAD).astype(BF16)
    g = _log_sigmoid(_dot(a_lr, wa2_ref[...]) + ba_ref[...]) * (1.0 / GATE_TAU)

    q_scr[...] = proj(wgla_ref, C_Q, GLA_DK)
    k_scr[...] = proj(wgla_ref, C_K, GLA_DK)
    p = proj(wpg_ref, C_P, POOL_WIDTH)

    _decay_factors(g, e_scr, ep_scr, es_scr, T, J)
    Ts = T // S
    for i in range(S):
        _pool_mix(p[i * Ts:(i + 1) * Ts], buf_ref[i], mixed_scr.at[i * Ts:(i + 1) * Ts], Ts)
        buf_ref[i] = p[(i + 1) * Ts - POOL_CARRY:(i + 1) * Ts, :]
    return xn


def _tile_wide_projections(xn, wgla_ref, wpg_ref, v_scr, r_scr, ga_scr, gb_scr):
    v_scr[...] = _dot(xn, wgla_ref[:, C_V:C_V + GLA_DV])
    r_scr[...] = _dot(xn, wgla_ref[:, C_R:C_R + GLA_DV])
    ga_scr[...] = _dot(xn, wpg_ref[:, C_GA:C_GA + D_MODEL])
    gb_scr[...] = _dot(xn, wpg_ref[:, C_GB:C_GB + D_MODEL])


def _tile_gates(wpool_ref, pscale_ref, r_scr, ga_scr, gb_scr, mixed_scr):
    for h in range(GLA_HEADS):
        ks = slice(h * DK_HEAD, (h + 1) * DK_HEAD)
        vs = slice(h * DV_HEAD, (h + 1) * DV_HEAD)
        r = r_scr[:, vs]
        r_scr[:, vs] = r * jax.nn.sigmoid(r)
        ga_scr[:, vs] = jax.nn.sigmoid(ga_scr[:, vs])
        y_b = _dot(mixed_scr[:, ks].astype(BF16), wpool_ref[h])
        gb_scr[:, vs] = jax.nn.sigmoid(gb_scr[:, vs]) * (y_b * pscale_ref[:, vs])


def _tile_attention(lvl_ref, e_scr, q_scr, k_scr, J):
    lvl = lvl_ref[...]
    atts = []
    for h in range(GLA_HEADS):
        ks = slice(h * DK_HEAD, (h + 1) * DK_HEAD)
        qh = q_scr[:, ks] * (DK_HEAD ** -0.5)
        kh = k_scr[:, ks]
        att = jnp.where(lvl == J, _dot_nt(qh.astype(BF16), kh.astype(BF16)), 0.0)
        for j in range(J):
            e = e_scr[j, :, ks]
            pj = _dot_nt((qh * e).astype(BF16), (kh * e).astype(BF16))
            att = jnp.where(lvl == j, pj, att)
        atts.append(att.astype(BF16))
    return atts


def _tile_back(x, atts, gng_ref, wpool_ref, pscale_ref, wout_ref, s_ref,
               e_scr, ep_scr, es_scr, q_scr, k_scr, v_scr, r_scr, ga_scr, gb_scr, mixed_scr, merged_scr, T, J, S):
    Ts = T // S
    streams = [slice(i * Ts, (i + 1) * Ts) for i in range(S)]
    for h in range(GLA_HEADS):
        ks = slice(h * DK_HEAD, (h + 1) * DK_HEAD)
        vs = slice(h * DV_HEAD, (h + 1) * DV_HEAD)
        qh = q_scr[:, ks] * (DK_HEAD ** -0.5)
        kh = k_scr[:, ks]
        vh = v_scr[:, vs].astype(BF16)

        q_dec = (qh * ep_scr[:, ks]).astype(BF16)
        o = jnp.concatenate([_dot(q_dec[rows], s_ref[i, h].astype(BF16)) for i, rows in enumerate(streams)], axis=0)
        o = o + _dot(atts[h], vh)

        k_dec = (kh * es_scr[:, ks]).astype(BF16)
        for i, rows in enumerate(streams):
            last = rows.stop - 1
            decay = jnp.broadcast_to(ep_scr[last:last + 1, ks], (DK_HEAD, DK_HEAD)).T
            decay = jnp.concatenate([decay] * (DV_HEAD // DK_HEAD), axis=1)
            s_ref[i, h] = decay * s_ref[i, h] + _dot_tn(k_dec[rows], vh[rows])

        o = o * jax.lax.rsqrt(jnp.mean(o * o, axis=-1, keepdims=True) + EPS) * gng_ref[:, vs]
        merged = ga_scr[:, vs] * (r_scr[:, vs] * o) + gb_scr[:, vs]
        merged_scr[:, vs] = merged.astype(BF16)

    return x + _dot(merged_scr[...], wout_ref[...])


def _mixer_kernel(x_ref, s0_ref, buf0_ref, n1_ref, wgla_ref, wgate_ref, wpg_ref, wa2_ref, ba_ref, gng_ref,
                  wpool_ref, pscale_ref, wout_ref, lvl_ref,
                  h_ref, s_ref, buf_ref, *scratch, T, J, U, S):
    @pl.when(pl.program_id(1) == 0)
    def _():
        s_ref[...] = s0_ref[...]
        buf_ref[...] = buf0_ref[...]

    tiles = [[scr.at[u] for scr in scratch] for u in range(U)]
    Ts = T // S
    tokens = [slice(u * Ts, (u + 1) * Ts) for u in range(U)]
    xns = [_tile_front(x_ref[:, tokens[u], :].reshape(T, D_MODEL), n1_ref, wgla_ref, wgate_ref, wpg_ref, wa2_ref,
                       ba_ref, buf_ref, *tiles[u], T, J, S) for u in range(U)]
    for u in range(U):
        _tile_wide_projections(xns[u], wgla_ref, wpg_ref, *tiles[u][5:9])
    for u in range(U):
        _tile_gates(wpool_ref, pscale_ref, *tiles[u][6:10])
    atts = [_tile_attention(lvl_ref, tiles[u][0], tiles[u][3], tiles[u][4], J) for u in range(U)]
    for u in range(U):
        out = _tile_back(x_ref[:, tokens[u], :].reshape(T, D_MODEL), atts[u], gng_ref, wpool_ref, pscale_ref,
                         wout_ref, s_ref, *tiles[u], T, J, S)
        h_ref[:, tokens[u], :] = out.reshape(S, Ts, D_MODEL)


def _const_spec(shape):
    zeros = (0,) * len(shape)
    return pl.BlockSpec(shape, lambda b, t: zeros, pipeline_mode=pl.Buffered(1))


def _mixer(x, s0, buf0, w, Ts, S, shared_state):
    B, L, D = x.shape
    assert L % Ts == 0 and Ts >= POOL_CARRY and B % S == 0 and (S == 1 or (L == Ts and not shared_state))
    T = S * Ts
    U = math.gcd(L // Ts, TILES_PER_STEP)
    J, lvl = _level_table(T, Ts)
    consts = (jnp.asarray(lvl),)

    state_idx = (lambda b, t: (0, 0, 0, 0)) if shared_state else (lambda b, t: (b, 0, 0, 0))
    buf_idx = (lambda b, t: (0, 0, 0)) if shared_state else (lambda b, t: (b, 0, 0))
    weights = (w["norm1_g"], w["w_in"], w["w_gate"], w["w_poolgate"], w["w_alpha2"], w["b_alpha"], w["gla_norm_g"],
               w["w_pool"], w["pool_scale"], w["w_out"]) + consts
    weight_blocks = [a.shape for a in weights]
    weight_blocks[1] = (D_MODEL, W_GLA_COLS)
    in_specs = [
        pl.BlockSpec((S, U * Ts, D), lambda b, t: (b, t, 0)),
        pl.BlockSpec((S, GLA_HEADS, DK_HEAD, DV_HEAD), state_idx),
        pl.BlockSpec((S, POOL_CARRY, POOL_WIDTH), buf_idx),
    ] + [_const_spec(shape) for shape in weight_blocks]
    out_shape = (
        jax.ShapeDtypeStruct((B, L, D), F32),
        jax.ShapeDtypeStruct((B, GLA_HEADS, DK_HEAD, DV_HEAD), F32),
        jax.ShapeDtypeStruct((B, POOL_CARRY, POOL_WIDTH), F32),
    )
    out_specs = (
        pl.BlockSpec((S, U * Ts, D), lambda b, t: (b, t, 0)),
        pl.BlockSpec((S, GLA_HEADS, DK_HEAD, DV_HEAD), lambda b, t: (b, 0, 0, 0)),
        pl.BlockSpec((S, POOL_CARRY, POOL_WIDTH), lambda b, t: (b, 0, 0)),
    )
    scratch = [
        pltpu.VMEM((U, J, T, GLA_DK), F32),
        pltpu.VMEM((U, T, GLA_DK), F32),
        pltpu.VMEM((U, T, GLA_DK), F32),
        pltpu.VMEM((U, T, GLA_DK), F32),
        pltpu.VMEM((U, T, GLA_DK), F32),
        pltpu.VMEM((U, T, GLA_DV), F32),
        pltpu.VMEM((U, T, GLA_DV), F32),
        pltpu.VMEM((U, T, D_MODEL), F32),
        pltpu.VMEM((U, T, D_MODEL), F32),
        pltpu.VMEM((U, T, POOL_WIDTH), F32),
        pltpu.VMEM((U, T, D_MODEL), BF16),
    ]
    return pl.pallas_call(
        functools.partial(_mixer_kernel, T=T, J=J, U=U, S=S),
        out_shape=out_shape,
        grid=(B // S, L // (U * Ts)),
        in_specs=in_specs,
        out_specs=out_specs,
        scratch_shapes=scratch,
        compiler_params=pltpu.CompilerParams(
            dimension_semantics=("arbitrary", "arbitrary"),
            vmem_limit_bytes=VMEM_LIMIT_BYTES),
        name=f"mixer_s{S}t{Ts}x{U}",
    )(x, s0, buf0, *weights)


def _ffn_kernel(h_ref, n2_ref, wg_ref, wu_ref, wd_ref, nf_ref, y_ref):
    R = h_ref.shape[0] // FFN_PIECES
    rows = [slice(i * R, (i + 1) * R) for i in range(FFN_PIECES)]
    hs = [h_ref[r, :] for r in rows]
    hns = [_rms(h, n2_ref[...]).astype(BF16) for h in hs]
    for i in range(FFN_PIECES):
        gate = _dot(hns[i], wg_ref[...])
        up = _dot(hns[i], wu_ref[...])
        act = (gate * jax.nn.sigmoid(gate) * up).astype(BF16)
        h2 = hs[i] + _dot(act, wd_ref[...])
        y_ref[rows[i], :] = _rms(h2, nf_ref[...])


def _ffn(h, w, R):
    N, D = h.shape
    assert N % R == 0
    weights = (w["norm2_g"], w["w_ffn_gate"], w["w_ffn_up"], w["w_ffn_down"], w["norm_f_g"])

    def const(a):
        zeros = (0,) * a.ndim
        return pl.BlockSpec(a.shape, lambda i: zeros, pipeline_mode=pl.Buffered(1))

    return pl.pallas_call(
        _ffn_kernel,
        out_shape=jax.ShapeDtypeStruct((N, D), F32),
        grid=(N // R,),
        in_specs=[pl.BlockSpec((R, D), lambda i: (i, 0))] + [const(a) for a in weights],
        out_specs=pl.BlockSpec((R, D), lambda i: (i, 0)),
        compiler_params=pltpu.CompilerParams(
            dimension_semantics=("arbitrary",),
            vmem_limit_bytes=VMEM_LIMIT_BYTES),
        name=f"ffn_r{R}",
    )(h, *weights)


def _pack_weights(norm1_g, w_in, w_alpha2, b_alpha, gla_norm_g, w_pool, pool_scale, w_out,
                  norm2_g, w_ffn_gate, w_ffn_up, w_ffn_down, norm_f_g):
    o_p = W_GLA_COLS + GATE_RANK
    w_in_bf = w_in.astype(BF16)
    w_gate = jnp.pad(w_in_bf[:, W_GLA_COLS:o_p], ((0, 0), (0, GATE_PAD - GATE_RANK)))
    w_poolgate = w_in_bf[:, o_p:]
    w_alpha2_packed = jnp.pad(w_alpha2, ((0, GATE_PAD - GATE_RANK), (0, 0))).astype(BF16)
    row = lambda v: v.reshape(1, -1).astype(F32)
    return dict(
        norm1_g=row(norm1_g), w_in=w_in_bf, w_gate=w_gate, w_poolgate=w_poolgate,
        w_alpha2=w_alpha2_packed, b_alpha=row(b_alpha),
        gla_norm_g=row(gla_norm_g), w_pool=w_pool.astype(BF16), pool_scale=row(pool_scale),
        w_out=w_out.astype(BF16), norm2_g=row(norm2_g), w_ffn_gate=w_ffn_gate.astype(BF16),
        w_ffn_up=w_ffn_up.astype(BF16), w_ffn_down=w_ffn_down.astype(BF16), norm_f_g=row(norm_f_g))


def _largest_tile(n, cap):
    t = cap
    while n % t:
        t //= 2
    return t


def kernel(x_prompt, x_sample, state_gla, state_pool, meta_tokens, norm1_g, w_in, w_alpha2, b_alpha,
           gla_norm_g, w_pool, pool_scale, w_out, norm2_g, w_ffn_gate, w_ffn_up, w_ffn_down, norm_f_g):
    w = _pack_weights(norm1_g, w_in, w_alpha2, b_alpha, gla_norm_g, w_pool, pool_scale, w_out,
                      norm2_g, w_ffn_gate, w_ffn_up, w_ffn_down, norm_f_g)
    B, L, D = x_prompt.shape
    Bs, Ls, _ = x_sample.shape
    n_meta = meta_tokens.shape[0]
    assert POOL_BUF <= n_meta <= META_TILE and PAST_LEN >= POOL_BUF

    s_zero = jnp.zeros((1, GLA_HEADS, DK_HEAD, DV_HEAD), F32)
    buf_zero = jnp.zeros((1, POOL_CARRY, POOL_WIDTH), F32)
    x_meta = jnp.pad(meta_tokens.astype(F32), ((META_TILE - n_meta, 0), (0, 0)))[None]
    _, s_meta, buf_meta = _mixer(x_meta, s_zero, buf_zero, w, META_TILE, 1, True)

    h_p, s_p, buf_p = _mixer(x_prompt, s_meta, buf_meta, w, _largest_tile(L, PROMPT_TILE), 1, True)

    buf_s0 = jnp.pad(state_pool.astype(F32), ((0, 0), (POOL_CARRY - POOL_BUF, 0), (0, 0)))
    ts = _largest_tile(Ls, PROMPT_TILE)
    streams = math.gcd(Bs, PROMPT_TILE // ts) if ts == Ls else 1
    h_s, s_s, buf_s = _mixer(x_sample, state_gla.astype(F32), buf_s0, w, ts, streams, False)

    y_p = _ffn(h_p.reshape(B * L, D), w, _largest_tile(B * L, FFN_TILE)).reshape(B, L, D)
    y_s = _ffn(h_s.reshape(Bs * Ls, D), w, _largest_tile(Bs * Ls, FFN_TILE)).reshape(Bs, Ls, D)
    return (y_p, y_s,
            s_p.astype(state_gla.dtype), buf_p[:, POOL_CARRY - POOL_BUF:].astype(state_pool.dtype),
            s_s.astype(state_gla.dtype), buf_s[:, POOL_CARRY - POOL_BUF:].astype(state_pool.dtype))
```

```python
import functools
import math

import numpy as np
import jax
import jax.numpy as jnp
from jax.experimental import pallas as pl
from jax.experimental.pallas import tpu as pltpu

D_MODEL = 1024
GLA_HEADS = 4
GLA_DK = D_MODEL // 2
GLA_DV = D_MODEL
DK_HEAD = GLA_DK // GLA_HEADS
DV_HEAD = GLA_DV // GLA_HEADS
GATE_RANK = 16
GATE_TAU = 16.0
POOL_WIDTH = D_MODEL // 2
POOL_WINDOWS = (2, 4, 8, 16)
POOL_GROUPS = len(POOL_WINDOWS)
POOL_GROUP = POOL_WIDTH // POOL_GROUPS
POOL_BUF = max(POOL_WINDOWS) - 1
POOL_CARRY = POOL_BUF + 1
PAST_LEN = 1024
EPS = 1e-6

LANES = 128
SUBLANES = 8
GATE_PAD = LANES

C_Q = 0
C_K = C_Q + GLA_DK
C_V = C_K + GLA_DK
C_R = C_V + GLA_DV
W_GLA_COLS = C_R + GLA_DV
C_P = 0
C_GA = C_P + POOL_WIDTH
C_GB = C_GA + D_MODEL

VMEM_LIMIT_BYTES = 56 * 1024 * 1024

PROMPT_TILE = 256
DECAY_STRIP = 256
TILES_PER_STEP = 2
META_TILE = 128
FFN_TILE = 512
FFN_PIECES = 2

F32 = jnp.float32
BF16 = jnp.bfloat16


def _level_table(T, block):
    J = int(math.log2(block))
    assert 1 << J == block and block >= 2 * SUBLANES and T % block == 0
    t = np.arange(T)[:, None]
    u = np.arange(T)[None, :]
    hb = np.floor(np.log2(np.maximum(t ^ u, 1))).astype(np.int32)
    lvl = np.where((u > t) | (t // block != u // block), -1, np.where(u == t, J, hb)).astype(np.int32)
    return J, lvl


def _rms(x, g):
    return x * jax.lax.rsqrt(jnp.mean(x * x, axis=-1, keepdims=True) + EPS) * g


def _dot(a, b):
    return jnp.dot(a, b, preferred_element_type=F32)


def _dot_nt(a, b):
    return jax.lax.dot_general(a, b, (((1,), (1,)), ((), ())), preferred_element_type=F32)


def _dot_tn(a, b):
    return jax.lax.dot_general(a, b, (((0,), (0,)), ((), ())), preferred_element_type=F32)


def _log_sigmoid(z):
    return jnp.minimum(z, 0.0) - jnp.log(1.0 + jnp.exp(-jnp.abs(z)))


def _decay_factors(g, e_scr, ep_scr, es_scr, T, J):
    assert SUBLANES == 8
    W = g.shape[1]
    a = jnp.exp(g).reshape(T // SUBLANES, SUBLANES, W)
    r = jax.lax.broadcasted_iota(jnp.int32, a.shape, 1)
    one = jnp.ones_like(a)

    def prev(x, s):
        return pltpu.roll(x, s, axis=1)

    def nxt(x, s):
        return pltpu.roll(x, SUBLANES - s, axis=1)

    s1 = a * jnp.where(r >= 1, prev(a, 1), one)
    s2 = s1 * jnp.where(r >= 2, prev(s1, 2), one)
    ep = s2 * jnp.where(r >= 4, prev(s2, 4), one)
    c = jnp.where(r <= 6, nxt(a, 1), one)
    d1 = c * jnp.where(r <= 6, nxt(c, 1), one)
    d2 = d1 * jnp.where(r <= 5, nxt(d1, 2), one)
    es = d2 * jnp.where(r <= 3, nxt(d2, 4), one)

    r4 = r & 3
    e0 = jnp.where((r & 1) == 1, a, one)
    e1 = jnp.where(r4 == 0, c, jnp.where(r4 == 1, one, jnp.where(r4 == 2, a, s1)))
    lower2 = jnp.where(r == 0, d1 * nxt(c, 2), jnp.where(r == 1, d1, jnp.where(r == 2, c, one)))
    upper2 = jnp.where(r == 4, a, jnp.where(r == 5, s1, jnp.where(r == 6, s1 * prev(a, 2), s2)))
    e2 = jnp.where(r <= 3, lower2, upper2)
    for j, e in enumerate((e0, e1, e2)):
        e_scr[j] = e.reshape(T, W)
    ep_scr[...] = ep.reshape(T, W)
    es_scr[...] = es.reshape(T, W)

    for j in range(3, J):
        m = 1 << j
        for lo in range(0, T, 2 * m):
            mid, hi = lo + m, lo + 2 * m
            e_scr[j, lo:mid] = es_scr[lo:mid]
            e_scr[j, mid:hi] = ep_scr[mid:hi]
            lower_total = ep_scr[mid - 1:mid, :]
            upper_total = ep_scr[hi - 1:hi, :]
            ep_scr[mid:hi] = ep_scr[mid:hi] * lower_total
            es_scr[lo:mid] = es_scr[lo:mid] * upper_total


def _pool_mix(p, carry, mixed_scr, T):
    n = (T + POOL_CARRY) // SUBLANES
    s = jnp.concatenate([carry, p], axis=0).reshape(n, SUBLANES, POOL_WIDTH)
    tok = s[POOL_CARRY // SUBLANES:]
    r = jax.lax.broadcasted_iota(jnp.int32, (n, SUBLANES, LANES), 1)

    def shifted(x, k):
        down = lambda y: jnp.concatenate([y[:1], y[:-1]], axis=0)
        if k == SUBLANES:
            return down(x)
        xr = pltpu.roll(x, k, axis=1)
        rr = jnp.concatenate([r] * (x.shape[-1] // LANES), axis=-1)
        return jnp.where(rr >= k, xr, down(xr))

    width = 1
    for gi, w in enumerate(POOL_WINDOWS):
        assert w == 2 * width and w <= POOL_CARRY
        s = s + shifted(s, width)
        width = w
        mixed = s[POOL_CARRY // SUBLANES:, :, :POOL_GROUP] * (1.0 / w) - tok[:, :, gi * POOL_GROUP:(gi + 1) * POOL_GROUP]
        mixed_scr[:, gi * POOL_GROUP:(gi + 1) * POOL_GROUP] = mixed.reshape(T, POOL_GROUP)
        s = s[:, :, POOL_GROUP:]


def _tile_front(x, n1_ref, wgla_ref, wgate_ref, wpg_ref, wa2_ref, ba_ref, buf_ref,
                e_scr, ep_scr, es_scr, q_scr, k_scr, v_scr, r_scr, ga_scr, gb_scr, mixed_scr, merged_scr, T, J, S):
    xn = _rms(x, n1_ref[...]).astype(BF16)

    def proj(w_ref, lo, width):
        return _dot(xn, w_ref[:, lo:lo + width])

    a_lr = proj(wgate_ref, 0, GATE_PAD).astype(BF16)
    g = _log_sigmoid(_dot(a_lr, wa2_ref[...]) + ba_ref[...]) * (1.0 / GATE_TAU)

    q_scr[...] = proj(wgla_ref, C_Q, GLA_DK)
    k_scr[...] = proj(wgla_ref, C_K, GLA_DK)
    p = proj(wpg_ref, C_P, POOL_WIDTH)

    for c0 in range(0, GLA_DK, DECAY_STRIP):
        cols = slice(c0, c0 + DECAY_STRIP)
        _decay_factors(g[:, cols], e_scr.at[:, :, cols], ep_scr.at[:, cols], es_scr.at[:, cols], T, J)
    Ts = T // S
    for i in range(S):
        _pool_mix(p[i * Ts:(i + 1) * Ts], buf_ref[i], mixed_scr.at[i * Ts:(i + 1) * Ts], Ts)
        buf_ref[i] = p[(i + 1) * Ts - POOL_CARRY:(i + 1) * Ts, :]
    return xn


def _tile_wide_projections(xn, wgla_ref, wpg_ref, v_scr, r_scr, ga_scr, gb_scr):
    v_scr[...] = _dot(xn, wgla_ref[:, C_V:C_V + GLA_DV])
    r_scr[...] = _dot(xn, wgla_ref[:, C_R:C_R + GLA_DV])
    ga_scr[...] = _dot(xn, wpg_ref[:, C_GA:C_GA + D_MODEL])
    gb_scr[...] = _dot(xn, wpg_ref[:, C_GB:C_GB + D_MODEL])


def _tile_gates(wpool_ref, pscale_ref, r_scr, ga_scr, gb_scr, mixed_scr):
    for h in range(GLA_HEADS):
        ks = slice(h * DK_HEAD, (h + 1) * DK_HEAD)
        vs = slice(h * DV_HEAD, (h + 1) * DV_HEAD)
        r = r_scr[:, vs]
        r_scr[:, vs] = r * jax.nn.sigmoid(r)
        ga_scr[:, vs] = jax.nn.sigmoid(ga_scr[:, vs])
        y_b = _dot(mixed_scr[:, ks].astype(BF16), wpool_ref[h])
        gb_scr[:, vs] = jax.nn.sigmoid(gb_scr[:, vs]) * (y_b * pscale_ref[:, vs])


def _tile_attention(lvl_ref, e_scr, q_scr, k_scr, J):
    lvl = lvl_ref[...]
    atts = []
    for h in range(GLA_HEADS):
        ks = slice(h * DK_HEAD, (h + 1) * DK_HEAD)
        qh = q_scr[:, ks] * (DK_HEAD ** -0.5)
        kh = k_scr[:, ks]
        att = jnp.where(lvl == J, _dot_nt(qh.astype(BF16), kh.astype(BF16)), 0.0)
        for j in range(J):
            e = e_scr[j, :, ks]
            pj = _dot_nt((qh * e).astype(BF16), (kh * e).astype(BF16))
            att = jnp.where(lvl == j, pj, att)
        atts.append(att.astype(BF16))
    return atts


def _tile_back(x, atts, gng_ref, wpool_ref, pscale_ref, wout_ref, s_ref,
               e_scr, ep_scr, es_scr, q_scr, k_scr, v_scr, r_scr, ga_scr, gb_scr, mixed_scr, merged_scr, T, J, S):
    Ts = T // S
    streams = [slice(i * Ts, (i + 1) * Ts) for i in range(S)]
    for h in range(GLA_HEADS):
        ks = slice(h * DK_HEAD, (h + 1) * DK_HEAD)
        vs = slice(h * DV_HEAD, (h + 1) * DV_HEAD)
        qh = q_scr[:, ks] * (DK_HEAD ** -0.5)
        kh = k_scr[:, ks]
        vh = v_scr[:, vs].astype(BF16)

        q_dec = (qh * ep_scr[:, ks]).astype(BF16)
        o = jnp.concatenate([_dot(q_dec[rows], s_ref[i, h].astype(BF16)) for i, rows in enumerate(streams)], axis=0)
        o = o + _dot(atts[h], vh)

        k_dec = (kh * es_scr[:, ks]).astype(BF16)
        for i, rows in enumerate(streams):
            last = rows.stop - 1
            decay = jnp.broadcast_to(ep_scr[last:last + 1, ks], (DK_HEAD, DK_HEAD)).T
            decay = jnp.concatenate([decay] * (DV_HEAD // DK_HEAD), axis=1)
            s_ref[i, h] = decay * s_ref[i, h] + _dot_tn(k_dec[rows], vh[rows])

        o = o * jax.lax.rsqrt(jnp.mean(o * o, axis=-1, keepdims=True) + EPS) * gng_ref[:, vs]
        merged = ga_scr[:, vs] * (r_scr[:, vs] * o) + gb_scr[:, vs]
        merged_scr[:, vs] = merged.astype(BF16)

    return x + _dot(merged_scr[...], wout_ref[...])


def _mixer_kernel(x_ref, s0_ref, buf0_ref, n1_ref, wgla_ref, wgate_ref, wpg_ref, wa2_ref, ba_ref, gng_ref,
                  wpool_ref, pscale_ref, wout_ref, lvl_ref,
                  h_ref, s_ref, buf_ref, *scratch, T, J, U, S):
    @pl.when(pl.program_id(1) == 0)
    def _():
        s_ref[...] = s0_ref[...]
        buf_ref[...] = buf0_ref[...]

    tiles = [[scr.at[u] for scr in scratch] for u in range(U)]
    Ts = T // S
    tokens = [slice(u * Ts, (u + 1) * Ts) for u in range(U)]
    xns = [_tile_front(x_ref[:, tokens[u], :].reshape(T, D_MODEL), n1_ref, wgla_ref, wgate_ref, wpg_ref, wa2_ref,
                       ba_ref, buf_ref, *tiles[u], T, J, S) for u in range(U)]
    for u in range(U):
        _tile_wide_projections(xns[u], wgla_ref, wpg_ref, *tiles[u][5:9])
    for u in range(U):
        _tile_gates(wpool_ref, pscale_ref, *tiles[u][6:10])
    atts = [_tile_attention(lvl_ref, tiles[u][0], tiles[u][3], tiles[u][4], J) for u in range(U)]
    for u in range(U):
        out = _tile_back(x_ref[:, tokens[u], :].reshape(T, D_MODEL), atts[u], gng_ref, wpool_ref, pscale_ref,
                         wout_ref, s_ref, *tiles[u], T, J, S)
        h_ref[:, tokens[u], :] = out.reshape(S, Ts, D_MODEL)


def _const_spec(shape):
    zeros = (0,) * len(shape)
    return pl.BlockSpec(shape, lambda b, t: zeros, pipeline_mode=pl.Buffered(1))


def _mixer(x, s0, buf0, w, Ts, S, shared_state):
    B, L, D = x.shape
    assert L % Ts == 0 and Ts >= POOL_CARRY and B % S == 0 and (S == 1 or (L == Ts and not shared_state))
    T = S * Ts
    U = math.gcd(L // Ts, TILES_PER_STEP)
    J, lvl = _level_table(T, Ts)
    consts = (jnp.asarray(lvl),)

    state_idx = (lambda b, t: (0, 0, 0, 0)) if shared_state else (lambda b, t: (b, 0, 0, 0))
    buf_idx = (lambda b, t: (0, 0, 0)) if shared_state else (lambda b, t: (b, 0, 0))
    weights = (w["norm1_g"], w["w_in"], w["w_gate"], w["w_poolgate"], w["w_alpha2"], w["b_alpha"], w["gla_norm_g"],
               w["w_pool"], w["pool_scale"], w["w_out"]) + consts
    weight_blocks = [a.shape for a in weights]
    weight_blocks[1] = (D_MODEL, W_GLA_COLS)
    in_specs = [
        pl.BlockSpec((S, U * Ts, D), lambda b, t: (b, t, 0)),
        pl.BlockSpec((S, GLA_HEADS, DK_HEAD, DV_HEAD), state_idx),
        pl.BlockSpec((S, POOL_CARRY, POOL_WIDTH), buf_idx),
    ] + [_const_spec(shape) for shape in weight_blocks]
    out_shape = (
        jax.ShapeDtypeStruct((B, L, D), F32),
        jax.ShapeDtypeStruct((B, GLA_HEADS, DK_HEAD, DV_HEAD), F32),
        jax.ShapeDtypeStruct((B, POOL_CARRY, POOL_WIDTH), F32),
    )
    out_specs = (
        pl.BlockSpec((S, U * Ts, D), lambda b, t: (b, t, 0)),
        pl.BlockSpec((S, GLA_HEADS, DK_HEAD, DV_HEAD), lambda b, t: (b, 0, 0, 0)),
        pl.BlockSpec((S, POOL_CARRY, POOL_WIDTH), lambda b, t: (b, 0, 0)),
    )
    scratch = [
        pltpu.VMEM((U, J, T, GLA_DK), F32),
        pltpu.VMEM((U, T, GLA_DK), F32),
        pltpu.VMEM((U, T, GLA_DK), F32),
        pltpu.VMEM((U, T, GLA_DK), F32),
        pltpu.VMEM((U, T, GLA_DK), F32),
        pltpu.VMEM((U, T, GLA_DV), F32),
        pltpu.VMEM((U, T, GLA_DV), F32),
        pltpu.VMEM((U, T, D_MODEL), F32),
        pltpu.VMEM((U, T, D_MODEL), F32),
        pltpu.VMEM((U, T, POOL_WIDTH), F32),
        pltpu.VMEM((U, T, D_MODEL), BF16),
    ]
    return pl.pallas_call(
        functools.partial(_mixer_kernel, T=T, J=J, U=U, S=S),
        out_shape=out_shape,
        grid=(B // S, L // (U * Ts)),
        in_specs=in_specs,
        out_specs=out_specs,
        scratch_shapes=scratch,
        compiler_params=pltpu.CompilerParams(
            dimension_semantics=("arbitrary", "arbitrary"),
            vmem_limit_bytes=VMEM_LIMIT_BYTES),
        name=f"mixer_s{S}t{Ts}x{U}",
    )(x, s0, buf0, *weights)


def _ffn_kernel(h_ref, n2_ref, wg_ref, wu_ref, wd_ref, nf_ref, y_ref):
    R = h_ref.shape[0] // FFN_PIECES
    rows = [slice(i * R, (i + 1) * R) for i in range(FFN_PIECES)]
    hs = [h_ref[r, :] for r in rows]
    hns = [_rms(h, n2_ref[...]).astype(BF16) for h in hs]
    for i in range(FFN_PIECES):
        gate = _dot(hns[i], wg_ref[...])
        up = _dot(hns[i], wu_ref[...])
        act = (gate * jax.nn.sigmoid(gate) * up).astype(BF16)
        h2 = hs[i] + _dot(act, wd_ref[...])
        y_ref[rows[i], :] = _rms(h2, nf_ref[...])


def _ffn(h, w, R):
    N, D = h.shape
    assert N % R == 0
    weights = (w["norm2_g"], w["w_ffn_gate"], w["w_ffn_up"], w["w_ffn_down"], w["norm_f_g"])

    def const(a):
        zeros = (0,) * a.ndim
        return pl.BlockSpec(a.shape, lambda i: zeros, pipeline_mode=pl.Buffered(1))

    return pl.pallas_call(
        _ffn_kernel,
        out_shape=jax.ShapeDtypeStruct((N, D), F32),
        grid=(N // R,),
        in_specs=[pl.BlockSpec((R, D), lambda i: (i, 0))] + [const(a) for a in weights],
        out_specs=pl.BlockSpec((R, D), lambda i: (i, 0)),
        compiler_params=pltpu.CompilerParams(
            dimension_semantics=("arbitrary",),
            vmem_limit_bytes=VMEM_LIMIT_BYTES),
        name=f"ffn_r{R}",
    )(h, *weights)


def _pack_weights(norm1_g, w_in, w_alpha2, b_alpha, gla_norm_g, w_pool, pool_scale, w_out,
                  norm2_g, w_ffn_gate, w_ffn_up, w_ffn_down, norm_f_g):
    o_p = W_GLA_COLS + GATE_RANK
    w_in_bf = w_in.astype(BF16)
    w_gate = jnp.pad(w_in_bf[:, W_GLA_COLS:o_p], ((0, 0), (0, GATE_PAD - GATE_RANK)))
    w_poolgate = w_in_bf[:, o_p:]
    w_alpha2_packed = jnp.pad(w_alpha2, ((0, GATE_PAD - GATE_RANK), (0, 0))).astype(BF16)
    row = lambda v: v.reshape(1, -1).astype(F32)
    return dict(
        norm1_g=row(norm1_g), w_in=w_in_bf, w_gate=w_gate, w_poolgate=w_poolgate,
        w_alpha2=w_alpha2_packed, b_alpha=row(b_alpha),
        gla_norm_g=row(gla_norm_g), w_pool=w_pool.astype(BF16), pool_scale=row(pool_scale),
        w_out=w_out.astype(BF16), norm2_g=row(norm2_g), w_ffn_gate=w_ffn_gate.astype(BF16),
        w_ffn_up=w_ffn_up.astype(BF16), w_ffn_down=w_ffn_down.astype(BF16), norm_f_g=row(norm_f_g))


def _largest_tile(n, cap):
    t = cap
    while n % t:
        t //= 2
    return t


def kernel(x_prompt, x_sample, state_gla, state_pool, meta_tokens, norm1_g, w_in, w_alpha2, b_alpha,
           gla_norm_g, w_pool, pool_scale, w_out, norm2_g, w_ffn_gate, w_ffn_up, w_ffn_down, norm_f_g):
    w = _pack_weights(norm1_g, w_in, w_alpha2, b_alpha, gla_norm_g, w_pool, pool_scale, w_out,
                      norm2_g, w_ffn_gate, w_ffn_up, w_ffn_down, norm_f_g)
    B, L, D = x_prompt.shape
    Bs, Ls, _ = x_sample.shape
    n_meta = meta_tokens.shape[0]
    assert POOL_BUF <= n_meta <= META_TILE and PAST_LEN >= POOL_BUF

    s_zero = jnp.zeros((1, GLA_HEADS, DK_HEAD, DV_HEAD), F32)
    buf_zero = jnp.zeros((1, POOL_CARRY, POOL_WIDTH), F32)
    x_meta = jnp.pad(meta_tokens.astype(F32), ((META_TILE - n_meta, 0), (0, 0)))[None]
    _, s_meta, buf_meta = _mixer(x_meta, s_zero, buf_zero, w, META_TILE, 1, True)

    h_p, s_p, buf_p = _mixer(x_prompt, s_meta, buf_meta, w, _largest_tile(L, PROMPT_TILE), 1, True)

    buf_s0 = jnp.pad(state_pool.astype(F32), ((0, 0), (POOL_CARRY - POOL_BUF, 0), (0, 0)))
    ts = _largest_tile(Ls, PROMPT_TILE)
    streams = math.gcd(Bs, PROMPT_TILE // ts) if ts == Ls else 1
    h_s, s_s, buf_s = _mixer(x_sample, state_gla.astype(F32), buf_s0, w, ts, streams, False)

    y_p = _ffn(h_p.reshape(B * L, D), w, _largest_tile(B * L, FFN_TILE)).reshape(B, L, D)
    y_s = _ffn(h_s.reshape(Bs * Ls, D), w, _largest_tile(Bs * Ls, FFN_TILE)).reshape(Bs, Ls, D)
    return (y_p, y_s,
            s_p.astype(state_gla.dtype), buf_p[:, POOL_CARRY - POOL_BUF:].astype(state_pool.dtype),
            s_s.astype(state_gla.dtype), buf_s[:, POOL_CARRY - POOL_BUF:].astype(state_pool.dtype))
```
